```python
import jax, jax.numpy as jnp
from jax import lax
import numpy as np

D_MODEL = 1024
BATCH = 8
SEQ = 2048
DEPTH = 2
DEC_BATCH = 128
DEC_SEQ = 1
PAST_LEN = 8192
PAGE_SIZE = 128

HEAD_DIM = 64
MLA_HEADS = 4
MLA_NOPE = 64
MLA_ROPE = 32
MLA_V = 64
MLA_Q_LORA = 384
MLA_KV_LORA = 256
NSA_HEADS = 8
NSA_KV_HEADS = 2
NSA_GROUP = NSA_HEADS // NSA_KV_HEADS
BLK_CMP = 32
BLK_SLC = 64
TOP_N = 8
WINDOW = 512
BLKS_PER_PAGE = PAGE_SIZE // BLK_SLC
FOX_HEADS = 4
FORGET_BIAS = 3.0
D_FF = 4 * D_MODEL
ROPE_THETA = 10000.0
EPS = 1e-6
Q_BLOCK = 128
N_BRANCH = 3
NEG_INF = -1e30
FORCE_SCORE = 1e4
IN_SPLITS = (MLA_Q_LORA, MLA_KV_LORA, MLA_ROPE, NSA_HEADS * HEAD_DIM, 6 * NSA_KV_HEADS * HEAD_DIM, 3 * NSA_HEADS, 3 * FOX_HEADS * HEAD_DIM, FOX_HEADS, N_BRANCH * D_MODEL)
IN_COLS = MLA_Q_LORA + MLA_KV_LORA + MLA_ROPE + NSA_HEADS * HEAD_DIM + 6 * NSA_KV_HEADS * HEAD_DIM + 3 * NSA_HEADS + 3 * FOX_HEADS * HEAD_DIM + FOX_HEADS + N_BRANCH * D_MODEL
STATE_NAMES = ('mla_latent', 'mla_krope', 'nsa_cmp_k', 'nsa_cmp_v', 'nsa_slc_k', 'nsa_slc_v', 'nsa_win_k', 'nsa_win_v', 'fox_k', 'fox_v', 'fox_logf')

kernel_name = 'hybrid_mla_nsa_fox_decoder_step'


def rms_norm(x, g):
    xf = x.astype(jnp.float32)
    y = xf * lax.rsqrt(jnp.mean(xf * xf, axis=-1, keepdims=True) + EPS)
    return (y * g.astype(jnp.float32)).astype(x.dtype)


def rope(x, pos):
    d = x.shape[-1]
    inv = ROPE_THETA ** (-jnp.arange(0, d, 2, dtype=jnp.float32) / d)
    ang = pos.astype(jnp.float32)[:, None] * inv[None, :]
    cos = jnp.cos(ang)[:, None, :]
    sin = jnp.sin(ang)[:, None, :]
    xf = x.astype(jnp.float32)
    x1, x2 = xf[..., : d // 2], xf[..., d // 2:]
    return jnp.concatenate([x1 * cos - x2 * sin, x1 * sin + x2 * cos], axis=-1).astype(x.dtype)


def masked_softmax(logits, mask):
    logits = jnp.where(mask, logits.astype(jnp.float32), NEG_INF)
    m = jnp.max(logits, axis=-1, keepdims=True)
    p = jnp.where(mask, jnp.exp(logits - m), 0.0)
    return p / jnp.maximum(jnp.sum(p, axis=-1, keepdims=True), 1e-30)


def sweep_query_blocks(fn, *qs):
    n_q = qs[0].shape[1]
    qb = Q_BLOCK if n_q % Q_BLOCK == 0 else n_q
    nb = n_q // qb
    blocks = tuple(jnp.moveaxis(a.reshape(a.shape[0], nb, qb, *a.shape[2:]), 1, 0) for a in qs)
    starts = jnp.arange(nb, dtype=jnp.int32) * qb
    out = lax.map(lambda xs: fn(xs[0], *xs[1:]), (starts,) + blocks)
    out = jnp.moveaxis(out, 0, 1)
    return out.reshape(out.shape[0], n_q, *out.shape[3:])


def project(h, pos, lw):
    B, S, _ = h.shape
    z = h @ lw['w_in']
    cuts = np.cumsum(IN_SPLITS)[:-1].tolist()
    cq, ckv, kr, nq, nkv, ng, fqkv, ff, mg = jnp.split(z, cuts, axis=-1)
    q = (rms_norm(cq, lw['g_mla_q']) @ lw['w_mla_uq']).reshape(B, S, MLA_HEADS, MLA_NOPE + MLA_ROPE)
    nkv = nkv.reshape(B, S, 6, NSA_KV_HEADS, HEAD_DIM)
    fqkv = fqkv.reshape(B, S, 3, FOX_HEADS, HEAD_DIM)
    logf = jax.nn.log_sigmoid(ff.astype(jnp.float32) + lw['b_forget'].astype(jnp.float32))
    return {
        'mla_q_nope': q[..., :MLA_NOPE],
        'mla_q_rope': rope(q[..., MLA_NOPE:], pos),
        'mla_latent': rms_norm(ckv, lw['g_mla_kv']),
        'mla_krope': rope(kr[:, :, None, :], pos)[:, :, 0],
        'nsa_q': rope(nq.reshape(B, S, NSA_HEADS, HEAD_DIM), pos),
        'nsa_cmp_k': rope(nkv[:, :, 0], pos), 'nsa_cmp_v': nkv[:, :, 1],
        'nsa_slc_k': rope(nkv[:, :, 2], pos), 'nsa_slc_v': nkv[:, :, 3],
        'nsa_win_k': rope(nkv[:, :, 4], pos), 'nsa_win_v': nkv[:, :, 5],
        'nsa_gate': jax.nn.sigmoid(ng.reshape(B, S, NSA_HEADS, 3)),
        'fox_q': fqkv[:, :, 0], 'fox_k': fqkv[:, :, 1], 'fox_v': fqkv[:, :, 2],
        'fox_logf': logf,
        'merge_gate': jax.nn.sigmoid(mg.reshape(B, S, N_BRANCH, D_MODEL)),
    }


def mla_attend(q_nope, q_rope, latent, krope, w_uk, w_uv, q_offset):
    L = latent.shape[1]
    scale = (MLA_NOPE + MLA_ROPE) ** -0.5
    q_lat = jnp.einsum('bqhn,chn->bqhc', q_nope, w_uk)
    kpos = jnp.arange(L)

    def block(q0, ql, qr):
        qpos = q_offset + q0 + jnp.arange(ql.shape[1])
        logits = (jnp.einsum('bqhc,blc->bhql', ql, latent) + jnp.einsum('bqhr,blr->bhql', qr, krope)) * scale
        p = masked_softmax(logits, kpos[None, :] <= qpos[:, None])
        o_lat = jnp.einsum('bhql,blc->bqhc', p.astype(latent.dtype), latent)
        return jnp.einsum('bqhc,chv->bqhv', o_lat, w_uv)

    o = sweep_query_blocks(block, q_lat, q_rope)
    return o.reshape(o.shape[0], o.shape[1], MLA_HEADS * MLA_V)


def fox_attend(q, k, v, logf_cum, q_offset):
    L = k.shape[1]
    n_q = q.shape[1]
    scale = HEAD_DIM ** -0.5
    cum_k = jnp.swapaxes(logf_cum, 1, 2)
    cum_q = logf_cum[:, q_offset:q_offset + n_q]
    kpos = jnp.arange(L)

    def block(q0, qblk, cq):
        qpos = q_offset + q0 + jnp.arange(qblk.shape[1])
        decay = jnp.swapaxes(cq, 1, 2)[..., None] - cum_k[:, :, None, :]
        logits = jnp.einsum('bqhd,blhd->bhql', qblk, k).astype(jnp.float32) * scale + decay
        p = masked_softmax(logits, kpos[None, :] <= qpos[:, None])
        return jnp.einsum('bhql,blhd->bqhd', p.astype(v.dtype), v)

    o = sweep_query_blocks(block, q, cum_q)
    return o.reshape(o.shape[0], n_q, FOX_HEADS * HEAD_DIM)


def contiguous_block_gather(k, v):
    B, L = k.shape[:2]
    kb = k.reshape(B, L // BLK_SLC, BLK_SLC, NSA_KV_HEADS, HEAD_DIM)
    vb = v.reshape(B, L // BLK_SLC, BLK_SLC, NSA_KV_HEADS, HEAD_DIM)
    b_ix = jnp.arange(B)[:, None, None, None]
    g_ix = jnp.arange(NSA_KV_HEADS)[None, None, :, None]

    def gather(idx):
        return kb[b_ix, idx, :, g_ix], vb[b_ix, idx, :, g_ix]
    return gather


def paged_block_gather(pool_k, pool_v, layer, page_table, new_k, new_v):
    B, S = new_k.shape[:2]
    nb_past = page_table.shape[1] * PAGE_SIZE // BLK_SLC
    nb_new = -(-S // BLK_SLC)
    pad = ((0, 0), (0, nb_new * BLK_SLC - S), (0, 0), (0, 0))
    nk = jnp.pad(new_k, pad).reshape(B, nb_new, BLK_SLC, NSA_KV_HEADS, HEAD_DIM)
    nv = jnp.pad(new_v, pad).reshape(B, nb_new, BLK_SLC, NSA_KV_HEADS, HEAD_DIM)
    b_ix = jnp.arange(B)[:, None, None, None]
    g_ix = jnp.arange(NSA_KV_HEADS)[None, None, :, None]
    rows = jnp.arange(BLK_SLC)

    def gather(idx):
        is_past = (idx < nb_past)[..., None, None]
        pidx = jnp.minimum(idx, nb_past - 1)
        page = page_table[b_ix, pidx // BLKS_PER_PAGE][..., None]
        row = (pidx % BLKS_PER_PAGE)[..., None] * BLK_SLC + rows
        gp = g_ix[..., None]
        kp = pool_k[layer, page, row, gp]
        vp = pool_v[layer, page, row, gp]
        nidx = jnp.maximum(idx - nb_past, 0)
        kn = nk[b_ix, nidx, :, g_ix]
        vn = nv[b_ix, nidx, :, g_ix]
        return jnp.where(is_past, kp.astype(kn.dtype), kn), jnp.where(is_past, vp.astype(vn.dtype), vn)
    return gather


def nsa_attend(q, gates, cmp_k, cmp_v, gather_slc, n_slc, win_k, win_v, phi_k, phi_v, q_offset):
    B = q.shape[0]
    L = cmp_k.shape[1]
    n_cmp = L // BLK_CMP
    n_sel = min(TOP_N, n_slc)
    ratio = BLK_SLC // BLK_CMP
    scale = HEAD_DIM ** -0.5

    def compress(x, phi):
        xb = x[:, :n_cmp * BLK_CMP].reshape(B, n_cmp, BLK_CMP, NSA_KV_HEADS, HEAD_DIM)
        xb = jnp.moveaxis(xb, 3, 2).reshape(B, n_cmp, NSA_KV_HEADS, BLK_CMP * HEAD_DIM)
        return xb @ phi

    kc = compress(cmp_k, phi_k)
    vc = compress(cmp_v, phi_v)
    cmp_end = (jnp.arange(n_cmp) + 1) * BLK_CMP - 1
    blk_ids = jnp.arange(n_slc)[None, :]

    def block(q0, qblk, gblk):
        qb = qblk.shape[1]
        qpos = q_offset + q0 + jnp.arange(qb)
        qg = qblk.reshape(B, qb, NSA_KV_HEADS, NSA_GROUP, HEAD_DIM)
        lc = jnp.einsum('bqghd,bngd->bqghn', qg, kc) * scale
        pc = masked_softmax(lc, (cmp_end[None, :] <= qpos[:, None])[:, None, None, :])
        o_cmp = jnp.einsum('bqghn,bngd->bqghd', pc.astype(vc.dtype), vc)
        imp = jnp.sum(pc, axis=3)
        imp = jnp.pad(imp, ((0, 0), (0, 0), (0, 0), (0, n_slc * ratio - n_cmp)))
        imp = imp.reshape(B, qb, NSA_KV_HEADS, n_slc, ratio).sum(-1)
        cur = (qpos // BLK_SLC)[:, None]
        forced = (blk_ids == 0) | (blk_ids == cur) | (blk_ids == cur - 1)
        imp = jnp.where(forced[None, :, None, :], FORCE_SCORE, jnp.where((blk_ids > cur)[None, :, None, :], -1.0, imp))
        _, idx = lax.top_k(imp, n_sel)
        ks, vs = gather_slc(idx)
        m = n_sel * BLK_SLC
        ls = jnp.einsum('bqghd,bqgnkd->bqghnk', qg, ks).reshape(B, qb, NSA_KV_HEADS, NSA_GROUP, m) * scale
        spos = idx[..., None] * BLK_SLC + jnp.arange(BLK_SLC)
        smask = (spos <= qpos[None, :, None, None, None]).reshape(B, qb, NSA_KV_HEADS, 1, m)
        ps = masked_softmax(ls, smask)
        o_slc = jnp.einsum('bqghm,bqgmd->bqghd', ps.astype(vs.dtype), vs.reshape(B, qb, NSA_KV_HEADS, m, HEAD_DIM))
        kw = lax.dynamic_slice_in_dim(win_k, q0, WINDOW + qb, axis=1)
        vw = lax.dynamic_slice_in_dim(win_v, q0, WINDOW + qb, axis=1)
        wpos = q_offset - WINDOW + q0 + jnp.arange(WINDOW + qb)
        diff = qpos[:, None] - wpos[None, :]
        wmask = (wpos[None, :] >= 0) & (diff >= 0) & (diff <= WINDOW)
        lwin = jnp.einsum('bqghd,bkgd->bqghk', qg, kw) * scale
        pw = masked_softmax(lwin, wmask[:, None, None, :])
        o_win = jnp.einsum('bqghk,bkgd->bqghd', pw.astype(vw.dtype), vw)
        g = gblk.reshape(B, qb, NSA_KV_HEADS, NSA_GROUP, 3)
        o = g[..., 0:1] * o_cmp + g[..., 1:2] * o_slc + g[..., 2:3] * o_win
        return o.reshape(B, qb, NSA_HEADS * HEAD_DIM)

    return sweep_query_blocks(block, q, gates)


def merge_branches(gate, o_mla, o_nsa, o_fox, lw):
    m = (gate[:, :, 0] * (o_mla @ lw['w_br_mla'])
         + gate[:, :, 1] * (o_nsa @ lw['w_br_nsa'])
         + gate[:, :, 2] * (o_fox @ lw['w_br_fox']))
    return m @ lw['w_out']


def sq_relu_ffn(h, w1, w2):
    return jnp.square(jax.nn.relu(h @ w1)) @ w2


def prompt_mixer(h, lw):
    B, S, _ = h.shape
    pos = jnp.arange(S, dtype=jnp.int32)
    pr = project(h, pos, lw)
    o_mla = mla_attend(pr['mla_q_nope'], pr['mla_q_rope'], pr['mla_latent'], pr['mla_krope'], lw['w_mla_uk'], lw['w_mla_uv'], 0)
    gather = contiguous_block_gather(pr['nsa_slc_k'], pr['nsa_slc_v'])
    pad = ((0, 0), (WINDOW, 0), (0, 0), (0, 0))
    o_nsa = nsa_attend(pr['nsa_q'], pr['nsa_gate'], pr['nsa_cmp_k'], pr['nsa_cmp_v'], gather, S // BLK_SLC,
                       jnp.pad(pr['nsa_win_k'], pad), jnp.pad(pr['nsa_win_v'], pad), lw['w_nsa_phi_k'], lw['w_nsa_phi_v'], 0)
    cum = jnp.cumsum(pr['fox_logf'], axis=1)
    o_fox = fox_attend(pr['fox_q'], pr['fox_k'], pr['fox_v'], cum, 0)
    mix = merge_branches(pr['merge_gate'], o_mla, o_nsa, o_fox, lw)
    keep = min(WINDOW, S)
    rows = {n: pr[n] for n in STATE_NAMES}
    rows['nsa_win_k'] = pr['nsa_win_k'][:, S - keep:]
    rows['nsa_win_v'] = pr['nsa_win_v'][:, S - keep:]
    return mix, rows


def sample_mixer(h, lw, layer, page_table, caches):
    B, S, _ = h.shape
    past = page_table.shape[1] * PAGE_SIZE
    pos = past + jnp.arange(S, dtype=jnp.int32)
    pr = project(h, pos, lw)

    def with_past(name):
        pool = caches[name]
        old = pool[layer, page_table].reshape(B, past, *pool.shape[3:])
        return jnp.concatenate([old.astype(pr[name].dtype), pr[name]], axis=1)

    o_mla = mla_attend(pr['mla_q_nope'], pr['mla_q_rope'], with_past('mla_latent'), with_past('mla_krope'),
                       lw['w_mla_uk'], lw['w_mla_uv'], past)
    gather = paged_block_gather(caches['nsa_slc_k'], caches['nsa_slc_v'], layer, page_table, pr['nsa_slc_k'], pr['nsa_slc_v'])
    n_slc = past // BLK_SLC + (-(-S // BLK_SLC))
    win_buf = caches['nsa_win_k'].shape[2]
    win_k = jnp.concatenate([caches['nsa_win_k'][layer].astype(pr['nsa_win_k'].dtype), pr['nsa_win_k']], axis=1)
    win_v = jnp.concatenate([caches['nsa_win_v'][layer].astype(pr['nsa_win_v'].dtype), pr['nsa_win_v']], axis=1)
    pad = ((0, 0), (WINDOW - win_buf, 0), (0, 0), (0, 0))
    o_nsa = nsa_attend(pr['nsa_q'], pr['nsa_gate'], with_past('nsa_cmp_k'), with_past('nsa_cmp_v'), gather, n_slc,
                       jnp.pad(win_k, pad), jnp.pad(win_v, pad), lw['w_nsa_phi_k'], lw['w_nsa_phi_v'], past)
    cum = jnp.cumsum(with_past('fox_logf').astype(jnp.float32), axis=1)
    o_fox = fox_attend(pr['fox_q'], with_past('fox_k'), with_past('fox_v'), cum, past)
    mix = merge_branches(pr['merge_gate'], o_mla, o_nsa, o_fox, lw)
    rows = {n: pr[n] for n in STATE_NAMES}
    rows['nsa_win_k'] = win_k[:, S:]
    rows['nsa_win_v'] = win_v[:, S:]
    return mix, rows


def setup_inputs(seed: int = 0) -> dict:
    key = jax.random.key(seed)
    keys = iter(jax.random.split(key, 64))

    def nrm(shape, scale=1.0):
        return jax.random.normal(next(keys), shape, jnp.float32) * scale

    def gain(shape):
        return 1.0 + nrm(shape, 0.02)

    n_pages = PAST_LEN // PAGE_SIZE
    n_pool = (5 * DEC_BATCH * n_pages) // 4
    win_buf = min(WINDOW, PAST_LEN)
    page_table = jax.random.permutation(next(keys), n_pool)[:DEC_BATCH * n_pages].reshape(DEC_BATCH, n_pages).astype(jnp.int32)
    pool = (DEPTH, n_pool, PAGE_SIZE)
    nsa_kv = pool + (NSA_KV_HEADS, HEAD_DIM)
    fox_kv = pool + (FOX_HEADS, HEAD_DIM)
    win = (DEPTH, DEC_BATCH, win_buf, NSA_KV_HEADS, HEAD_DIM)
    d = D_MODEL
    return {
        'x_prompt': nrm((BATCH, SEQ, d)),
        'x_sample': nrm((DEC_BATCH, DEC_SEQ, d)),
        'cache_mla_latent': nrm(pool + (MLA_KV_LORA,)),
        'cache_mla_krope': nrm(pool + (MLA_ROPE,)),
        'cache_nsa_cmp_k': nrm(nsa_kv),
        'cache_nsa_cmp_v': nrm(nsa_kv),
        'cache_nsa_slc_k': nrm(nsa_kv),
        'cache_nsa_slc_v': nrm(nsa_kv),
        'state_nsa_win_k': nrm(win),
        'state_nsa_win_v': nrm(win),
        'cache_fox_k': nrm(fox_kv),
        'cache_fox_v': nrm(fox_kv),
        'cache_fox_logf': jax.nn.log_sigmoid(FORGET_BIAS + nrm(pool + (FOX_HEADS,))),
        'page_table': page_table,
        'w_in': nrm((DEPTH, d, IN_COLS), d ** -0.5),
        'b_forget': FORGET_BIAS + nrm((DEPTH, FOX_HEADS), 0.1),
        'g_mla_q': gain((DEPTH, MLA_Q_LORA)),
        'g_mla_kv': gain((DEPTH, MLA_KV_LORA)),
        'w_mla_uq': nrm((DEPTH, MLA_Q_LORA, MLA_HEADS * (MLA_NOPE + MLA_ROPE)), MLA_Q_LORA ** -0.5),
        'w_mla_uk': nrm((DEPTH, MLA_KV_LORA, MLA_HEADS, MLA_NOPE), MLA_KV_LORA ** -0.5),
        'w_mla_uv': nrm((DEPTH, MLA_KV_LORA, MLA_HEADS, MLA_V), MLA_KV_LORA ** -0.5),
        'w_nsa_phi_k': nrm((DEPTH, BLK_CMP * HEAD_DIM, HEAD_DIM), (BLK_CMP * HEAD_DIM) ** -0.5),
        'w_nsa_phi_v': nrm((DEPTH, BLK_CMP * HEAD_DIM, HEAD_DIM), (BLK_CMP * HEAD_DIM) ** -0.5),
        'w_br_mla': nrm((DEPTH, MLA_HEADS * MLA_V, d), (MLA_HEADS * MLA_V) ** -0.5),
        'w_br_nsa': nrm((DEPTH, NSA_HEADS * HEAD_DIM, d), (NSA_HEADS * HEAD_DIM) ** -0.5),
        'w_br_fox': nrm((DEPTH, FOX_HEADS * HEAD_DIM, d), (FOX_HEADS * HEAD_DIM) ** -0.5),
        'w_out': nrm((DEPTH, d, d), d ** -0.5),
        'g_norm_mix': gain((DEPTH, d)),
        'g_norm_ffn': gain((DEPTH, d)),
        'w_ff1': nrm((DEPTH, d, D_FF), d ** -0.5),
        'w_ff2': nrm((DEPTH, D_FF, d), D_FF ** -0.5),
        'g_norm_final': gain((d,)),
    }


def reference(x_prompt, x_sample, cache_mla_latent, cache_mla_krope, cache_nsa_cmp_k, cache_nsa_cmp_v,
              cache_nsa_slc_k, cache_nsa_slc_v, state_nsa_win_k, state_nsa_win_v, cache_fox_k, cache_fox_v,
              cache_fox_logf, page_table, w_in, b_forget, g_mla_q, g_mla_kv, w_mla_uq, w_mla_uk, w_mla_uv,
              w_nsa_phi_k, w_nsa_phi_v, w_br_mla, w_br_nsa, w_br_fox, w_out, g_norm_mix, g_norm_ffn,
              w_ff1, w_ff2, g_norm_final):
    caches = {
        'mla_latent': cache_mla_latent, 'mla_krope': cache_mla_krope,
        'nsa_cmp_k': cache_nsa_cmp_k, 'nsa_cmp_v': cache_nsa_cmp_v,
        'nsa_slc_k': cache_nsa_slc_k, 'nsa_slc_v': cache_nsa_slc_v,
        'nsa_win_k': state_nsa_win_k, 'nsa_win_v': state_nsa_win_v,
        'fox_k': cache_fox_k, 'fox_v': cache_fox_v, 'fox_logf': cache_fox_logf,
    }
    new_p = {n: [] for n in STATE_NAMES}
    new_s = {n: [] for n in STATE_NAMES}
    xp, xs = x_prompt, x_sample
    for layer in range(DEPTH):
        lw = {
            'w_in': w_in[layer], 'b_forget': b_forget[layer], 'g_mla_q': g_mla_q[layer], 'g_mla_kv': g_mla_kv[layer],
            'w_mla_uq': w_mla_uq[layer], 'w_mla_uk': w_mla_uk[layer], 'w_mla_uv': w_mla_uv[layer],
            'w_nsa_phi_k': w_nsa_phi_k[layer], 'w_nsa_phi_v': w_nsa_phi_v[layer],
            'w_br_mla': w_br_mla[layer], 'w_br_nsa': w_br_nsa[layer], 'w_br_fox': w_br_fox[layer], 'w_out': w_out[layer],
        }
        mix_p, rows_p = prompt_mixer(rms_norm(xp, g_norm_mix[layer]), lw)
        xp = xp + mix_p
        xp = xp + sq_relu_ffn(rms_norm(xp, g_norm_ffn[layer]), w_ff1[layer], w_ff2[layer])
        mix_s, rows_s = sample_mixer(rms_norm(xs, g_norm_mix[layer]), lw, layer, page_table, caches)
        xs = xs + mix_s
        xs = xs + sq_relu_ffn(rms_norm(xs, g_norm_ffn[layer]), w_ff1[layer], w_ff2[layer])
        for n in STATE_NAMES:
            new_p[n].append(rows_p[n])
            new_s[n].append(rows_s[n])
    y_prompt = rms_norm(xp, g_norm_final)
    y_sample = rms_norm(xs, g_norm_final)
    return (y_prompt, y_sample,
            jnp.stack(new_p['mla_latent']), jnp.stack(new_s['mla_latent']),
            jnp.stack(new_p['mla_krope']), jnp.stack(new_s['mla_krope']),
            jnp.stack(new_p['nsa_cmp_k']), jnp.stack(new_s['nsa_cmp_k']),
            jnp.stack(new_p['nsa_cmp_v']), jnp.stack(new_s['nsa_cmp_v']),
            jnp.stack(new_p['nsa_slc_k']), jnp.stack(new_s['nsa_slc_k']),
            jnp.stack(new_p['nsa_slc_v']), jnp.stack(new_s['nsa_slc_v']),
            jnp.stack(new_p['nsa_win_k']), jnp.stack(new_s['nsa_win_k']),
            jnp.stack(new_p['nsa_win_v']), jnp.stack(new_s['nsa_win_v']),
            jnp.stack(new_p['fox_k']), jnp.stack(new_s['fox_k']),
            jnp.stack(new_p['fox_v']), jnp.stack(new_s['fox_v']),
            jnp.stack(new_p['fox_logf']), jnp.stack(new_s['fox_logf']))
```

```python
import functools

import jax
import jax.numpy as jnp
from jax import lax
from jax.experimental import pallas as pl
from jax.experimental.pallas import tpu as pltpu

D_MODEL = 1024
PAGE_SIZE = 128
HEAD_DIM = 64
MLA_HEADS = 4
MLA_NOPE = 64
MLA_ROPE = 32
MLA_V = 64
MLA_Q_LORA = 384
MLA_KV_LORA = 256
NSA_HEADS = 8
NSA_KV_HEADS = 2
NSA_GROUP = NSA_HEADS // NSA_KV_HEADS
BLK_CMP = 32
BLK_SLC = 64
TOP_N = 8
WINDOW = 512
FOX_HEADS = 4
D_FF = 4 * D_MODEL
ROPE_THETA = 10000.0
EPS = 1e-6
N_BRANCH = 3
NEG_INF = -1e30
FORCE_SCORE = 1e4

NSA_Q = NSA_HEADS * HEAD_DIM
NSA_KV = NSA_KV_HEADS * HEAD_DIM
FOX_D = FOX_HEADS * HEAD_DIM
MLA_QCAT = MLA_KV_LORA + MLA_HEADS * MLA_ROPE
CMP_FLAT = BLK_CMP * NSA_KV
TAB_W = 2 * NSA_Q + 2 * MLA_HEADS * MLA_ROPE

VMEM_LIMIT_BYTES = 56 * 1024 * 1024
ATT_TILE = 256
F32 = jnp.float32
BF16 = jnp.bfloat16


def _cparams(n_axes):
    return pltpu.CompilerParams(dimension_semantics=("arbitrary",) * n_axes,
                                vmem_limit_bytes=VMEM_LIMIT_BYTES)


def _rms(x, g):
    return x * lax.rsqrt(jnp.mean(x * x, axis=-1, keepdims=True) + EPS) * g


def _dot(a, b):
    return jnp.dot(a, b, preferred_element_type=F32)


def _dot_nt(a, b):
    return lax.dot_general(a, b, (((1,), (1,)), ((), ())), preferred_element_type=F32)


def _split3(x):
    hi = x.astype(BF16)
    r = x - hi.astype(F32)
    mid = r.astype(BF16)
    lo = (r - mid.astype(F32)).astype(BF16)
    return hi, mid, lo


def _dot_sel(x, e):
    hi, mid, lo = _split3(x)
    return _dot(hi, e) + _dot(mid, e) + _dot(lo, e)


def _iota(shape, dim):
    return lax.broadcasted_iota(jnp.int32, shape, dim)


def _proj_kernel(x_ref, gmix_ref, wa_ref, wkr_ref, wsm_ref, wnq_ref, wnkv_ref, wfox_ref, gq_ref, gkv_ref,
                 wuq_ref, wuk_ref, bias_ref, tab_ref,
                 qmla_ref, kcat_ref, lat_ref, krope_ref, nq_ref,
                 cmpk_ref, cmpv_ref, slck_ref, slcv_ref, wink_ref, winv_ref,
                 ngate_ref, fq_ref, fk_ref, fv_ref, logf_ref):
    x = x_ref[...]
    hb = _rms(x, gmix_ref[...]).astype(BF16)
    tab = tab_ref[...]
    cos64, sin64 = tab[:, :NSA_Q], tab[:, NSA_Q:2 * NSA_Q]
    cos32, sin32 = tab[:, 2 * NSA_Q:2 * NSA_Q + 128], tab[:, 2 * NSA_Q + 128:]

    za = _dot(hb, wa_ref[...])
    cqn = _rms(za[:, :MLA_Q_LORA], gq_ref[...]).astype(BF16)
    zq = _dot(cqn, wuq_ref[...])
    qlat = _dot(zq[:, :256].astype(BF16), wuk_ref[...])
    qrope = zq[:, 256:384] * cos32 + zq[:, 384:512] * sin32
    lane = _iota(qrope.shape, 1)
    for h in range(MLA_HEADS):
        qmla_ref[:, h * MLA_QCAT:h * MLA_QCAT + 256] = qlat[:, h * 256:(h + 1) * 256].astype(BF16)
        own = (lane >= h * MLA_ROPE) & (lane < (h + 1) * MLA_ROPE)
        qmla_ref[:, h * MLA_QCAT + 256:(h + 1) * MLA_QCAT] = jnp.where(own, qrope, 0.0).astype(BF16)
    lat = _rms(za[:, MLA_Q_LORA:], gkv_ref[...])
    lat_ref[...] = lat
    zkr = _dot(hb, wkr_ref[...])
    kr4 = zkr[:, :128] * cos32 + zkr[:, 128:] * sin32
    krope_ref[...] = kr4[:, :MLA_ROPE]
    kcat_ref[:, :256] = lat.astype(BF16)
    kcat_ref[:, 256:] = kr4.astype(BF16)
    zs = _dot(hb, wsm_ref[...])
    ngate_ref[...] = jax.nn.sigmoid(zs[:, 0:24])
    logf_ref[...] = jax.nn.log_sigmoid(zs[:, 24:28] + bias_ref[...])

    zn = _dot(hb, wnq_ref[...])
    nq_ref[...] = ((zn[:, :NSA_Q] * cos64 + zn[:, NSA_Q:] * sin64) * (HEAD_DIM ** -0.5)).astype(BF16)
    zk = _dot(hb, wnkv_ref[...])
    c2, s2 = cos64[:, :NSA_KV], sin64[:, :NSA_KV]
    for i, ref in enumerate((cmpk_ref, cmpv_ref, slck_ref, slcv_ref, wink_ref, winv_ref)):
        z = zk[:, i * NSA_KV:(i + 1) * NSA_KV]
        if i % 2 == 0:
            zr = zk[:, (6 + i // 2) * NSA_KV:(7 + i // 2) * NSA_KV]
            z = z * c2 + zr * s2
        ref[...] = z

    zf = _dot(hb, wfox_ref[...])
    fq_ref[...] = (zf[:, :FOX_D] * (HEAD_DIM ** -0.5)).astype(BF16)
    fk_ref[...] = zf[:, FOX_D:2 * FOX_D]
    fv_ref[...] = zf[:, 2 * FOX_D:]


def _project(x, tab, tab_rows_per_block, pw, tm):
    T = x.shape[0]
    nt = T // tm
    n_tab_blocks = tab.shape[0] // tm
    row = lambda i: (i, 0)
    const = lambda i: (0, 0)
    tab_map = (lambda i: (i % n_tab_blocks, 0)) if n_tab_blocks > 1 else const
    weights = [pw['gmix'], pw['wa'], pw['wkr'], pw['wsm'], pw['wnq'], pw['wnkv'], pw['wfox'], pw['gq'], pw['gkv'],
               pw['wuq'], pw['wuk'], pw['bias']]
    in_specs = ([pl.BlockSpec((tm, D_MODEL), row)]
                + [pl.BlockSpec(w.shape, const) for w in weights]
                + [pl.BlockSpec((tm, TAB_W), tab_map)])
    widths = [(MLA_HEADS * MLA_QCAT, BF16), (MLA_QCAT, BF16), (MLA_KV_LORA, F32), (MLA_ROPE, F32), (NSA_Q, BF16)]
    widths += [(NSA_KV, F32)] * 6
    widths += [(3 * NSA_HEADS, F32), (FOX_D, BF16), (FOX_D, F32), (FOX_D, F32), (FOX_HEADS, F32)]
    out_shape = [jax.ShapeDtypeStruct((T, w), dt) for w, dt in widths]
    out_specs = [pl.BlockSpec((tm, w), row) for w, _ in widths]
    outs = pl.pallas_call(
        _proj_kernel, grid=(nt,), in_specs=in_specs, out_specs=out_specs, out_shape=out_shape,
        compiler_params=_cparams(1), name="proj")(x, *weights, tab)
    names = ('qmla', 'kcat', 'mla_latent', 'mla_krope', 'nq', 'nsa_cmp_k', 'nsa_cmp_v', 'nsa_slc_k', 'nsa_slc_v',
             'nsa_win_k', 'nsa_win_v', 'ngate', 'fq', 'fox_k', 'fox_v', 'fox_logf')
    return dict(zip(names, outs))


def _flash_update(s, v, m_ref, l_ref, acc_ref, mask=None):
    m_prev = m_ref[...]
    m_new = jnp.maximum(m_prev, jnp.max(s, axis=-1, keepdims=True))
    alpha = jnp.exp(m_prev - m_new)
    p = jnp.exp(s - m_new)
    if mask is not None:
        p = jnp.where(mask, p, 0.0)
    l_ref[...] = alpha * l_ref[...] + jnp.sum(p, axis=-1, keepdims=True)
    acc_ref[...] = alpha * acc_ref[...] + _dot(p.astype(BF16), v)
    m_ref[...] = m_new


def _flash_init(m_ref, l_ref, acc_ref):
    m_ref[...] = jnp.full(m_ref.shape, NEG_INF, F32)
    l_ref[...] = jnp.zeros(l_ref.shape, F32)
    acc_ref[...] = jnp.zeros(acc_ref.shape, F32)


def _mla_prompt_kernel(q_ref, kcat_ref, wuv_ref, o_ref, m_ref, l_ref, acc_ref, *, t, scale):
    i = pl.program_id(1)
    q = jnp.concatenate([q_ref[:, h * MLA_QCAT:(h + 1) * MLA_QCAT] for h in range(MLA_HEADS)], axis=0)
    _flash_init(m_ref, l_ref, acc_ref)

    def step(j, causal):
        k = kcat_ref[pl.ds(pl.multiple_of(j * t, t), t), :]
        s = _dot_nt(q, k) * scale
        if causal:
            qpos = _iota(s.shape, 0) & (t - 1)
            s = jnp.where(_iota(s.shape, 1) <= qpos, s, NEG_INF)
        _flash_update(s, k[:, :MLA_KV_LORA], m_ref, l_ref, acc_ref)

    def body(j, c):
        step(j, False)
        return c
    lax.fori_loop(0, i, body, 0)
    step(i, True)
    o_lat = (acc_ref[...] / l_ref[...]).astype(BF16)
    out = _dot(o_lat[0:t], wuv_ref[0])
    for h in range(1, MLA_HEADS):
        out = out + _dot(o_lat[h * t:(h + 1) * t], wuv_ref[h])
    o_ref[...] = out.astype(BF16)


def _mla_prompt(qmla, kcat, wuv, B, S):
    t = ATT_TILE
    nq = S // t
    kern = functools.partial(_mla_prompt_kernel, t=t, scale=(MLA_NOPE + MLA_ROPE) ** -0.5)
    return pl.pallas_call(
        kern, grid=(B, nq),
        in_specs=[pl.BlockSpec((t, MLA_HEADS * MLA_QCAT), lambda b, i: (b * nq + i, 0)),
                  pl.BlockSpec((S, MLA_QCAT), lambda b, i: (b, 0)),
                  pl.BlockSpec(wuv.shape, lambda b, i: (0, 0, 0))],
        out_specs=pl.BlockSpec((t, MLA_HEADS * MLA_V), lambda b, i: (b * nq + i, 0)),
        out_shape=jax.ShapeDtypeStruct((B * S, MLA_HEADS * MLA_V), BF16),
        scratch_shapes=[pltpu.VMEM((MLA_HEADS * t, 1), F32), pltpu.VMEM((MLA_HEADS * t, 1), F32),
                        pltpu.VMEM((MLA_HEADS * t, MLA_KV_LORA), F32)],
        compiler_params=_cparams(2), name="mla_prompt")(qmla, kcat, wuv)


def _fox_prompt_kernel(q_ref, k_ref, v_ref, cq_ref, ck_ref, o_ref, m_ref, l_ref, acc_ref, *, t):
    i = pl.program_id(1)
    q = q_ref[...]
    cq = cq_ref[...]
    head_of_lane = lax.shift_right_logical(_iota(q.shape, 1), 6)
    qh = [jnp.where(head_of_lane == h, q, jnp.zeros_like(q)) for h in range(FOX_HEADS)]
    m_ref[...] = jnp.full(m_ref.shape, NEG_INF, F32)
    l_ref[...] = jnp.zeros(l_ref.shape, F32)
    acc_ref[...] = jnp.zeros(acc_ref.shape, F32)

    def step(j, causal):
        rows = pl.ds(pl.multiple_of(j * t, t), t)
        k = k_ref[rows, :].astype(BF16)
        v = v_ref[rows, :].astype(BF16)
        ck = ck_ref[0, j]
        for h in range(FOX_HEADS):
            s = _dot_nt(qh[h], k) + (cq[:, h:h + 1] - ck[h:h + 1, :])
            if causal:
                s = jnp.where(_iota(s.shape, 1) <= _iota(s.shape, 0), s, NEG_INF)
            _flash_update(s, v, m_ref.at[h], l_ref.at[h], acc_ref.at[h])

    def body(j, c):
        step(j, False)
        return c
    lax.fori_loop(0, i, body, 0)
    step(i, True)
    out = jnp.zeros(q.shape, F32)
    for h in range(FOX_HEADS):
        out = jnp.where(head_of_lane == h, acc_ref[h] / l_ref[h], out)
    o_ref[...] = out.astype(BF16)


def _fox_prompt(fq, fk, fv, cum_q, cum_k, B, S):
    t = ATT_TILE
    nq = S // t
    tile = lambda b, i: (b * nq + i, 0)
    seq = lambda b, i: (b, 0)
    return pl.pallas_call(
        functools.partial(_fox_prompt_kernel, t=t), grid=(B, nq),
        in_specs=[pl.BlockSpec((t, FOX_D), tile), pl.BlockSpec((S, FOX_D), seq), pl.BlockSpec((S, FOX_D), seq),
                  pl.BlockSpec((t, FOX_HEADS), tile),
                  pl.BlockSpec((1, nq, FOX_HEADS, t), lambda b, i: (b, 0, 0, 0))],
        out_specs=pl.BlockSpec((t, FOX_D), tile),
        out_shape=jax.ShapeDtypeStruct((B * S, FOX_D), BF16),
        scratch_shapes=[pltpu.VMEM((FOX_HEADS, t, 1), F32), pltpu.VMEM((FOX_HEADS, t, 1), F32),
                        pltpu.VMEM((FOX_HEADS, t, FOX_D), F32)],
        compiler_params=_cparams(2), name="fox_prompt")(fq, fk, fv, cum_q, cum_k)


def _compress_kernel(xk_ref, xv_ref, pk_ref, pv_ref, kc_ref, vc_ref):
    kc_ref[...] = _dot(xk_ref[...].astype(BF16), pk_ref[...])
    vc_ref[...] = _dot(xv_ref[...].astype(BF16), pv_ref[...])


def _compress(xk, xv, phik, phiv):
    n = xk.shape[0]
    tr = min(n, 128)
    row = lambda i: (i, 0)
    const = lambda i: (0, 0)
    return pl.pallas_call(
        _compress_kernel, grid=(n // tr,),
        in_specs=[pl.BlockSpec((tr, CMP_FLAT), row), pl.BlockSpec((tr, CMP_FLAT), row),
                  pl.BlockSpec(phik.shape, const), pl.BlockSpec(phiv.shape, const)],
        out_specs=[pl.BlockSpec((tr, NSA_KV), row)] * 2,
        out_shape=[jax.ShapeDtypeStruct((n, NSA_KV), F32)] * 2,
        compiler_params=_cparams(1), name="nsa_compress")(xk, xv, phik, phiv)


def _topk_mask(vals, k):
    n = vals.shape[-1]
    lane = _iota(vals.shape, 1).astype(F32)
    sel = jnp.zeros(vals.shape, F32)
    for _ in range(k):
        m = jnp.max(vals, axis=-1, keepdims=True)
        first = jnp.min(jnp.where(vals == m, lane, float(n)), axis=-1, keepdims=True)
        hit = lane == first
        sel = jnp.where(hit, 1.0, sel)
        vals = jnp.where(hit, -2.0, vals)
    return sel


def _slots_from_groups(o0, o1, t):
    low = _iota((t, NSA_KV), 1) < HEAD_DIM
    return jnp.concatenate([jnp.where(low, o0[j * t:(j + 1) * t], o1[j * t:(j + 1) * t])
                            for j in range(NSA_GROUP)], axis=1)


def _nsa_prompt_kernel(q_ref, gate_ref, kc_ref, vc_ref, sk_ref, sv_ref, wk_ref, wv_ref,
                       egate_ref, pair_ref, eblk_ref, o_ref, m_ref, l_ref, acc_ref, *, t, n_cmp, n_slc):
    i = pl.program_id(1)
    q = q_ref[...]
    zero = jnp.zeros((t, NSA_KV), BF16)
    half = lax.shift_right_logical(_iota((t, NSA_KV), 1), 6)
    rows4 = (NSA_GROUP * t, 1)
    qpos4 = i * t + (_iota(rows4, 0) & (t - 1))
    qpos = i * t + _iota((t, 1), 0)
    kc = kc_ref[...].astype(BF16)
    vc = vc_ref[...].astype(BF16)
    outs = [[], [], []]
    for g in range(NSA_KV_HEADS):
        qg = jnp.concatenate([jnp.where(half == g, q[:, j * NSA_KV:(j + 1) * NSA_KV], zero)
                              for j in range(NSA_GROUP)], axis=0)
        s = _dot_nt(qg, kc)
        cmp_end = (_iota(s.shape, 1) + 1) * BLK_CMP - 1
        vis = cmp_end <= qpos4
        s = jnp.where(vis, s, NEG_INF)
        p = jnp.where(vis, jnp.exp(s - jnp.max(s, axis=-1, keepdims=True)), 0.0)
        pc = p / jnp.maximum(jnp.sum(p, axis=-1, keepdims=True), 1e-30)
        outs[0].append(_dot(pc.astype(BF16), vc))
        imp = pc[0:t] + pc[t:2 * t] + pc[2 * t:3 * t] + pc[3 * t:4 * t]
        imp = _dot_sel(imp, pair_ref[...])
        blk = _iota(imp.shape, 1)
        cur = lax.shift_right_logical(qpos, 6)
        forced = (blk == 0) | (blk == cur) | (blk == cur - 1)
        imp = jnp.where(forced, FORCE_SCORE, jnp.where(blk > cur, -1.0, imp))
        sel = _topk_mask(imp, min(TOP_N, n_slc)).astype(BF16)
        sel4 = jnp.concatenate([sel] * NSA_GROUP, axis=0)

        _flash_init(m_ref, l_ref, acc_ref)

        def slc_step(j, c):
            rows = pl.ds(pl.multiple_of(j * t, t), t)
            k = sk_ref[rows, :].astype(BF16)
            v = sv_ref[rows, :].astype(BF16)
            s = _dot_nt(qg, k)
            kpos = j * t + _iota(s.shape, 1)
            mask = (_dot(sel4, eblk_ref[j]) > 0.5) & (kpos <= qpos4)
            _flash_update(jnp.where(mask, s, NEG_INF), v, m_ref, l_ref, acc_ref, mask)
            return c
        lax.fori_loop(0, i + 1, slc_step, 0)
        outs[1].append(acc_ref[...] / jnp.maximum(l_ref[...], 1e-30))

        _flash_init(m_ref, l_ref, acc_ref)

        def win_step(j, c):
            rows = pl.ds(pl.multiple_of(j * t, t), t)
            k = wk_ref[rows, :].astype(BF16)
            v = wv_ref[rows, :].astype(BF16)
            s = _dot_nt(qg, k)
            diff = qpos4 - (j * t + _iota(s.shape, 1))
            mask = (diff >= 0) & (diff <= WINDOW)
            _flash_update(jnp.where(mask, s, NEG_INF), v, m_ref, l_ref, acc_ref, mask)
            return c
        lax.fori_loop(jnp.maximum(i - WINDOW // t, 0), i + 1, win_step, 0)
        outs[2].append(acc_ref[...] / jnp.maximum(l_ref[...], 1e-30))

    gexp = _dot_sel(gate_ref[...], egate_ref[...])
    out = jnp.zeros((t, NSA_Q), F32)
    for br in range(3):
        out = out + gexp[:, br * NSA_Q:(br + 1) * NSA_Q] * _slots_from_groups(outs[br][0], outs[br][1], t)
    o_ref[...] = out.astype(BF16)


def _nsa_prompt(nq, ngate, kc, vc, sk, sv, wk, wv, consts, B, S):
    t = ATT_TILE
    nt = S // t
    n_cmp = S // BLK_CMP
    n_slc = S // BLK_SLC
    tile = lambda b, i: (b * nt + i, 0)
    seq = lambda b, i: (b, 0)
    c2 = lambda b, i: (0, 0)
    c3 = lambda b, i: (0, 0, 0)
    kern = functools.partial(_nsa_prompt_kernel, t=t, n_cmp=n_cmp, n_slc=n_slc)
    return pl.pallas_call(
        kern, grid=(B, nt),
        in_specs=[pl.BlockSpec((t, NSA_Q), tile), pl.BlockSpec((t, 3 * NSA_HEADS), tile),
                  pl.BlockSpec((n_cmp, NSA_KV), seq), pl.BlockSpec((n_cmp, NSA_KV), seq),
                  pl.BlockSpec((S, NSA_KV), seq), pl.BlockSpec((S, NSA_KV), seq),
                  pl.BlockSpec((S, NSA_KV), seq), pl.BlockSpec((S, NSA_KV), seq),
                  pl.BlockSpec(consts['egate'].shape, c2), pl.BlockSpec(consts['pair'].shape, c2),
                  pl.BlockSpec(consts['eblk'].shape, c3)],
        out_specs=pl.BlockSpec((t, NSA_Q), tile),
        out_shape=jax.ShapeDtypeStruct((B * S, NSA_Q), BF16),
        scratch_shapes=[pltpu.VMEM((NSA_GROUP * t, 1), F32), pltpu.VMEM((NSA_GROUP * t, 1), F32),
                        pltpu.VMEM((NSA_GROUP * t, NSA_KV), F32)],
        compiler_params=_cparams(2), name="nsa_prompt")(
            nq, ngate, kc, vc, sk, sv, wk, wv, consts['egate'], consts['pair'], consts['eblk'])


def _merge_kernel(x_ref, gmix_ref, wmg_ref, omla_ref, onsa_ref, ofox_ref, wbm_ref, wbn_ref, wbf_ref,
                  wout_ref, o_ref):
    x = x_ref[...]
    hb = _rms(x, gmix_ref[...]).astype(BF16)
    acc = None
    for br, (o_r, w_r) in enumerate(((omla_ref, wbm_ref), (onsa_ref, wbn_ref), (ofox_ref, wbf_ref))):
        gate = jax.nn.sigmoid(_dot(hb, wmg_ref[:, br * D_MODEL:(br + 1) * D_MODEL]))
        term = gate * _dot(o_r[...], w_r[...])
        acc = term if acc is None else acc + term
    o_ref[...] = x + _dot(acc.astype(BF16), wout_ref[...])


def _merge(x, o_mla, o_nsa, o_fox, mw, tm):
    T = x.shape[0]
    row = lambda i: (i, 0)
    const = lambda i: (0, 0)
    weights_a = [mw['gmix'], mw['wmg']]
    weights_b = [mw['wbm'], mw['wbn'], mw['wbf'], mw['wout']]
    return pl.pallas_call(
        _merge_kernel, grid=(T // tm,),
        in_specs=([pl.BlockSpec((tm, D_MODEL), row)] + [pl.BlockSpec(w.shape, const) for w in weights_a]
                  + [pl.BlockSpec((tm, o.shape[1]), row) for o in (o_mla, o_nsa, o_fox)]
                  + [pl.BlockSpec(w.shape, const) for w in weights_b]),
        out_specs=pl.BlockSpec((tm, D_MODEL), row),
        out_shape=jax.ShapeDtypeStruct((T, D_MODEL), F32),
        compiler_params=_cparams(1), name="merge")(x, *weights_a, o_mla, o_nsa, o_fox, *weights_b)


FF_CHUNK = 1024


def _ffn_kernel(x_ref, g_ref, w1_ref, w2_ref, gfin_ref, o_ref, y_ref):
    x = x_ref[...]
    hb = _rms(x, g_ref[...]).astype(BF16)
    acc = x
    for c in range(D_FF // FF_CHUNK):
        u = _dot(hb, w1_ref[:, c * FF_CHUNK:(c + 1) * FF_CHUNK])
        u = jnp.square(jnp.maximum(u, 0.0)).astype(BF16)
        acc = acc + _dot(u, w2_ref[c * FF_CHUNK:(c + 1) * FF_CHUNK, :])
    o_ref[...] = acc
    y_ref[...] = _rms(acc, gfin_ref[...])


def _ffn(x, g, w1, w2, gfin, tm):
    T = x.shape[0]
    row = lambda i: (i, 0)
    const = lambda i: (0, 0)
    return pl.pallas_call(
        _ffn_kernel, grid=(T // tm,),
        in_specs=[pl.BlockSpec((tm, D_MODEL), row), pl.BlockSpec(g.shape, const), pl.BlockSpec(w1.shape, const),
                  pl.BlockSpec(w2.shape, const), pl.BlockSpec(gfin.shape, const)],
        out_specs=[pl.BlockSpec((tm, D_MODEL), row)] * 2,
        out_shape=[jax.ShapeDtypeStruct((T, D_MODEL), F32)] * 2,
        compiler_params=_cparams(1), name="ffn")(x, g, w1, w2, gfin)


def _page_specs(block, layer, n_seq, n_pages, pages_per_step, reverse=False):
    def spec(i):
        def index_map(b, c, pt):
            cc = (n_pages // pages_per_step - 1 - c) if reverse else c
            slot = jnp.clip(b * n_pages + cc * pages_per_step + i, 0, n_seq * n_pages - 1)
            return (layer, pt[slot]) + (0,) * (len(block) - 2)
        return pl.BlockSpec(block, index_map)
    return [spec(i) for i in range(pages_per_step)]


def _mla_decode_kernel(pt_ref, q_ref, latn_ref, krn_ref, wuv_ref, *rest, P, scale):
    lat_refs, kr_refs = rest[:P], rest[P:2 * P]
    o_ref, m_ref, l_ref, acc_ref = rest[2 * P:]
    c = pl.program_id(1)

    @pl.when(c == 0)
    def _():
        _flash_init(m_ref, l_ref, acc_ref)

    q = q_ref[0]
    ql = q[:, :MLA_KV_LORA]
    qm = q[:, MLA_KV_LORA:].astype(F32)
    qr = (qm[:, 0:32] + qm[:, 32:64] + qm[:, 64:96] + qm[:, 96:128])
    qrb = qr.astype(BF16)
    ks = [lat_refs[i][...].astype(BF16) for i in range(P)]
    s = jnp.concatenate([_dot_nt(ql, ks[i]) + _dot(qrb, kr_refs[i][...].astype(BF16)) for i in range(P)],
                        axis=1) * scale
    m_prev = m_ref[...]
    m_new = jnp.maximum(m_prev, jnp.max(s, axis=-1, keepdims=True))
    alpha = jnp.exp(m_prev - m_new)
    p = jnp.exp(s - m_new)
    l_ref[...] = alpha * l_ref[...] + jnp.sum(p, axis=-1, keepdims=True)
    pb = p.astype(BF16)
    pv = _dot(pb[:, 0:PAGE_SIZE], ks[0])
    for i in range(1, P):
        pv = pv + _dot(pb[:, i * PAGE_SIZE:(i + 1) * PAGE_SIZE], ks[i])
    acc_ref[...] = alpha * acc_ref[...] + pv
    m_ref[...] = m_new

    @pl.when(c == pl.num_programs(1) - 1)
    def _():
        kn = latn_ref[0]
        s_new = (jnp.sum(ql.astype(F32) * kn, axis=-1, keepdims=True)
                 + jnp.sum(qr * krn_ref[0], axis=-1, keepdims=True)) * scale
        m_prev = m_ref[...]
        m_new = jnp.maximum(m_prev, s_new)
        alpha = jnp.exp(m_prev - m_new)
        p_new = jnp.exp(s_new - m_new)
        l = alpha * l_ref[...] + p_new
        o_lat = ((alpha * acc_ref[...] + p_new * kn) / l).astype(BF16)
        row = _iota((8, MLA_HEADS * MLA_V), 0)
        out = jnp.zeros((8, MLA_HEADS * MLA_V), F32)
        for h in range(MLA_HEADS):
            out = out + jnp.where(row == h, _dot(o_lat, wuv_ref[h]), 0.0)
        o_ref[0] = jnp.sum(out, axis=0, keepdims=True).astype(BF16)


def _mla_decode(layer, page_table, qmla, lat_new, kr_new, wuv, cache_lat, cache_kr, P):
    B, n_pages = page_table.shape
    q3 = jnp.pad(qmla.reshape(B, MLA_HEADS, MLA_QCAT), ((0, 0), (0, 8 - MLA_HEADS), (0, 0)))
    per_seq = lambda b, c, pt: (b, 0, 0)
    in_specs = ([pl.BlockSpec((1, 8, MLA_QCAT), per_seq),
                 pl.BlockSpec((1, 1, MLA_KV_LORA), per_seq), pl.BlockSpec((1, 1, MLA_ROPE), per_seq),
                 pl.BlockSpec(wuv.shape, lambda b, c, pt: (0, 0, 0))]
                + _page_specs((None, None, PAGE_SIZE, MLA_KV_LORA), layer, B, n_pages, P)
                + _page_specs((None, None, MLA_ROPE, PAGE_SIZE), layer, B, n_pages, P))
    grid_spec = pltpu.PrefetchScalarGridSpec(
        num_scalar_prefetch=1, grid=(B, n_pages // P), in_specs=in_specs,
        out_specs=pl.BlockSpec((1, 1, MLA_HEADS * MLA_V), per_seq),
        scratch_shapes=[pltpu.VMEM((8, 1), F32), pltpu.VMEM((8, 1), F32), pltpu.VMEM((8, MLA_KV_LORA), F32)])
    kern = functools.partial(_mla_decode_kernel, P=P, scale=(MLA_NOPE + MLA_ROPE) ** -0.5)
    out = pl.pallas_call(
        kern, grid_spec=grid_spec, out_shape=jax.ShapeDtypeStruct((B, 1, MLA_HEADS * MLA_V), BF16),
        compiler_params=_cparams(2), name="mla_decode")(
            page_table.reshape(-1), q3, lat_new.reshape(B, 1, -1), kr_new.reshape(B, 1, -1), wuv,
            *([cache_lat] * P), *([cache_kr] * P))
    return out.reshape(B, -1)


def _fox_decode_kernel(pt_ref, q_ref, kn_ref, vn_ref, fn_ref, tri_ref, *rest, P):
    k_refs, v_refs, f_refs = rest[:P], rest[P:2 * P], rest[2 * P:3 * P]
    o_ref, m_ref, l_ref, acc_ref, carry_ref = rest[3 * P:]
    c = pl.program_id(1)
    rowh = _iota((8, FOX_D), 0)
    head_of_lane = lax.shift_right_logical(_iota((8, FOX_D), 1), 6)
    own = rowh == head_of_lane
    qbd = jnp.where(own, q_ref[0].astype(F32), 0.0).astype(BF16)

    @pl.when(c == 0)
    def _():
        _flash_init(m_ref, l_ref, acc_ref)
        carry_ref[...] = fn_ref[0]

    carry = carry_ref[...]
    s_list = []
    for i in reversed(range(P)):
        f = jnp.concatenate([f_refs[i][...], jnp.zeros((8 - FOX_HEADS, PAGE_SIZE), F32)], axis=0)
        decay = _dot_sel(f, tri_ref[...]) + carry
        s_list.append(_dot(qbd, k_refs[i][...].astype(BF16)) + decay)
        carry = carry + jnp.sum(f, axis=-1, keepdims=True)
    carry_ref[...] = carry
    s = jnp.concatenate(s_list, axis=1)
    m_prev = m_ref[...]
    m_new = jnp.maximum(m_prev, jnp.max(s, axis=-1, keepdims=True))
    alpha = jnp.exp(m_prev - m_new)
    p = jnp.exp(s - m_new)
    l_ref[...] = alpha * l_ref[...] + jnp.sum(p, axis=-1, keepdims=True)
    pb = p.astype(BF16)
    pv = None
    for n, i in enumerate(reversed(range(P))):
        t = _dot_nt(pb[:, n * PAGE_SIZE:(n + 1) * PAGE_SIZE], v_refs[i][...].astype(BF16))
        pv = t if pv is None else pv + t
    acc_ref[...] = alpha * acc_ref[...] + pv
    m_ref[...] = m_new

    @pl.when(c == pl.num_programs(1) - 1)
    def _():
        s_new = jnp.sum(qbd.astype(F32) * kn_ref[0], axis=-1, keepdims=True)
        m_prev = m_ref[...]
        m_new = jnp.maximum(m_prev, s_new)
        alpha = jnp.exp(m_prev - m_new)
        p_new = jnp.exp(s_new - m_new)
        l = alpha * l_ref[...] + p_new
        o = (alpha * acc_ref[...] + p_new * vn_ref[0]) / l
        o_ref[0] = jnp.sum(jnp.where(own, o, 0.0), axis=0, keepdims=True).astype(BF16)


def _fox_decode(layer, page_table, fq, k_new, v_new, f_new, tri, cache_k, cache_v, cache_ft, P):
    B, n_pages = page_table.shape
    per_seq = lambda b, c, pt: (b, 0, 0)
    in_specs = ([pl.BlockSpec((1, 1, FOX_D), per_seq)] * 3
                + [pl.BlockSpec((1, 8, 1), per_seq), pl.BlockSpec(tri.shape, lambda b, c, pt: (0, 0))]
                + _page_specs((None, None, FOX_D, PAGE_SIZE), layer, B, n_pages, P, reverse=True)
                + _page_specs((None, None, FOX_D, PAGE_SIZE), layer, B, n_pages, P, reverse=True)
                + _page_specs((None, None, FOX_HEADS, PAGE_SIZE), layer, B, n_pages, P, reverse=True))
    grid_spec = pltpu.PrefetchScalarGridSpec(
        num_scalar_prefetch=1, grid=(B, n_pages // P), in_specs=in_specs,
        out_specs=pl.BlockSpec((1, 1, FOX_D), per_seq),
        scratch_shapes=[pltpu.VMEM((8, 1), F32), pltpu.VMEM((8, 1), F32), pltpu.VMEM((8, FOX_D), F32),
                        pltpu.VMEM((8, 1), F32)])
    f_new8 = jnp.pad(f_new.reshape(B, FOX_HEADS, 1), ((0, 0), (0, 8 - FOX_HEADS), (0, 0)))
    out = pl.pallas_call(
        functools.partial(_fox_decode_kernel, P=P), grid_spec=grid_spec,
        out_shape=jax.ShapeDtypeStruct((B, 1, FOX_D), BF16),
        compiler_params=_cparams(2), name="fox_decode")(
            page_table.reshape(-1), fq.reshape(B, 1, -1), k_new.reshape(B, 1, -1), v_new.reshape(B, 1, -1),
            f_new8, tri, *([cache_k] * P), *([cache_v] * P), *([cache_ft] * P))
    return out.reshape(B, -1)


def _group_rows(q_row, g):
    half = lax.shift_right_logical(_iota((1, NSA_KV), 1), 6)
    qf = q_row.astype(F32)
    rows = [jnp.where(half == g, qf[:, j * NSA_KV:(j + 1) * NSA_KV], 0.0) for j in range(NSA_GROUP)]
    return jnp.concatenate(rows + [jnp.zeros((8 - NSA_GROUP, NSA_KV), F32)], axis=0).astype(BF16)


def _nsa_cmp_decode_kernel(pt_ref, q_ref, pk_ref, pv_ref, pair_ref, perm_ref, *rest, P, n_cmp, n_sel):
    xk_refs, xv_refs = rest[:P], rest[P:2 * P]
    ocmp_ref, sel_ref, kc_ref, vc_ref = rest[2 * P:]
    c = pl.program_id(1)
    per_page = PAGE_SIZE // BLK_CMP
    rows = pl.ds(pl.multiple_of(c * (P * per_page), P * per_page), P * per_page)
    perm = perm_ref[...]
    for refs, phi_ref, dst in ((xk_refs, pk_ref, kc_ref), (xv_refs, pv_ref, vc_ref)):
        xs = []
        for i in range(0, P, 2):
            pair = jnp.concatenate([refs[i][...], refs[i + 1][...]], axis=1).astype(BF16)
            xs.append(_dot_nt(perm, pair))
        acc = None
        for r in range(0, BLK_CMP, 2):
            lhs = jnp.concatenate(
                [jnp.concatenate([x[rr * 8:(rr + 1) * 8] for x in xs], axis=0) for rr in (r, r + 1)], axis=1)
            t = _dot(lhs.astype(BF16), phi_ref[r * NSA_KV:(r + 2) * NSA_KV, :])
            acc = t if acc is None else acc + t
        dst[rows, :] = acc

    @pl.when(c == pl.num_programs(1) - 1)
    def _():
        kc = kc_ref[...].astype(BF16)
        vc = vc_ref[...].astype(BF16)
        q_row = q_ref[0]
        lane = _iota((1, n_cmp // 2), 1)
        lane_f = lane.astype(F32)
        for g in range(NSA_KV_HEADS):
            qg = _group_rows(q_row, g)
            s = _dot_nt(qg, kc)
            p = jnp.exp(s - jnp.max(s, axis=-1, keepdims=True))
            pc = p / jnp.maximum(jnp.sum(p, axis=-1, keepdims=True), 1e-30)
            ocmp_ref[0, g] = _dot(pc.astype(BF16), vc)
            imp = jnp.sum(pc[0:NSA_GROUP], axis=0, keepdims=True)
            imp = _dot_sel(imp, pair_ref[...])
            vals = jnp.where((lane == 0) | (lane == n_cmp // 2 - 1), FORCE_SCORE, imp)
            picks = jnp.zeros((1, 128), jnp.int32)
            for it in range(n_sel):
                m = jnp.max(vals, axis=-1, keepdims=True)
                first = jnp.min(jnp.where(vals == m, lane_f, float(n_cmp)), axis=-1, keepdims=True)
                vals = jnp.where(lane_f == first, -2.0, vals)
                picks = jnp.where(_iota((1, 128), 1) == it, first.astype(jnp.int32), picks)
            sel_ref[0, g] = picks


def _nsa_cmp_decode(layer, page_table, nq, phik, phiv, pair, perm, cache_kt, cache_vt, P, n_sel):
    B, n_pages = page_table.shape
    n_cmp = n_pages * PAGE_SIZE // BLK_CMP
    per_seq = lambda b, c, pt: (b, 0, 0)
    const = lambda b, c, pt: (0, 0)
    in_specs = ([pl.BlockSpec((1, 1, NSA_Q), per_seq), pl.BlockSpec(phik.shape, const),
                 pl.BlockSpec(phiv.shape, const), pl.BlockSpec(pair.shape, const), pl.BlockSpec(perm.shape, const)]
                + _page_specs((None, None, NSA_KV, PAGE_SIZE), layer, B, n_pages, P)
                + _page_specs((None, None, NSA_KV, PAGE_SIZE), layer, B, n_pages, P))
    grid_spec = pltpu.PrefetchScalarGridSpec(
        num_scalar_prefetch=1, grid=(B, n_pages // P), in_specs=in_specs,
        out_specs=[pl.BlockSpec((1, NSA_KV_HEADS, 8, NSA_KV), lambda b, c, pt: (b, 0, 0, 0)),
                   pl.BlockSpec((1, NSA_KV_HEADS, 1, 128), lambda b, c, pt: (b, 0, 0, 0))],
        scratch_shapes=[pltpu.VMEM((n_cmp, NSA_KV), F32), pltpu.VMEM((n_cmp, NSA_KV), F32)])
    kern = functools.partial(_nsa_cmp_decode_kernel, P=P, n_cmp=n_cmp, n_sel=n_sel)
    return pl.pallas_call(
        kern, grid_spec=grid_spec,
        out_shape=[jax.ShapeDtypeStruct((B, NSA_KV_HEADS, 8, NSA_KV), F32),
                   jax.ShapeDtypeStruct((B, NSA_KV_HEADS, 1, 128), jnp.int32)],
        compiler_params=_cparams(2), name="nsa_cmp_decode")(
            page_table.reshape(-1), nq.reshape(B, 1, -1), phik, phiv, pair, perm,
            *([cache_kt] * P), *([cache_vt] * P))


def _softmax_parts(parts):
    m = parts[0].max(axis=-1, keepdims=True)
    for s in parts[1:]:
        m = jnp.maximum(m, s.max(axis=-1, keepdims=True))
    es = [jnp.exp(s - m) for s in parts]
    tot = es[0].sum(axis=-1, keepdims=True)
    for e in es[1:]:
        tot = tot + e.sum(axis=-1, keepdims=True)
    inv = 1.0 / jnp.maximum(tot, 1e-30)
    return [e * inv for e in es]


def _nsa_sel_decode_kernel(pt_ref, sel_ref, q_ref, gate_ref, ocmp_ref, skn_ref, svn_ref, wkn_ref, wvn_ref,
                           wknt_ref, wvnt_ref, wk_ref, wv_ref, egate_ref, *rest, n_sel):
    nb = NSA_KV_HEADS * n_sel
    k_refs, v_refs = rest[:nb], rest[nb:2 * nb]
    o_ref, wko_ref, wvo_ref = rest[2 * nb:]
    b = pl.program_id(0)
    q_row = q_ref[0]
    skn, svn, wkn, wvn = skn_ref[0], svn_ref[0], wkn_ref[0], wvn_ref[0]
    wk = wk_ref[0, 0]
    wv = wv_ref[0, 0]
    key_half = lax.shift_right_logical(_iota((8, PAGE_SIZE), 1), 6)
    o_slc, o_win = [], []
    for g in range(NSA_KV_HEADS):
        qg = _group_rows(q_row, g)
        qf = qg.astype(F32)
        parts = []
        for n in range(n_sel):
            blk = sel_ref[b * nb + g * n_sel + n]
            s = _dot(qg, k_refs[g * n_sel + n][...].astype(BF16))
            parts.append(jnp.where(key_half == (blk & 1), s, NEG_INF))
        parts.append(jnp.sum(qf * skn, axis=-1, keepdims=True))
        ps = _softmax_parts(parts)
        o = ps[n_sel] * svn
        for n in range(n_sel):
            o = o + _dot_nt(ps[n].astype(BF16), v_refs[g * n_sel + n][...].astype(BF16))
        o_slc.append(o)
        parts = [_dot(qg, wk.astype(BF16)), jnp.sum(qf * wkn, axis=-1, keepdims=True)]
        ps = _softmax_parts(parts)
        o_win.append(_dot_nt(ps[0].astype(BF16), wv.astype(BF16)) + ps[1] * wvn)
    gexp = _dot_sel(gate_ref[0], egate_ref[...])
    low = _iota((1, NSA_KV), 1) < HEAD_DIM

    def slots(o0, o1):
        return jnp.concatenate([jnp.where(low, o0[j:j + 1], o1[j:j + 1]) for j in range(NSA_GROUP)], axis=1)
    out = (gexp[:, 0:NSA_Q] * slots(ocmp_ref[0, 0], ocmp_ref[0, 1])
           + gexp[:, NSA_Q:2 * NSA_Q] * slots(o_slc[0], o_slc[1])
           + gexp[:, 2 * NSA_Q:] * slots(o_win[0], o_win[1]))
    o_ref[0] = out.astype(BF16)
    nw = wk.shape[1]
    last = _iota(wk.shape, 1) == nw - 1
    mine = _iota(wknt_ref.shape, 1) == b
    for src, newt_ref, dst in ((wk, wknt_ref, wko_ref), (wv, wvnt_ref, wvo_ref)):
        col = jnp.sum(jnp.where(mine, newt_ref[...], 0.0), axis=1, keepdims=True)
        dst[0] = jnp.where(last, col, pltpu.roll(src, nw - 1, axis=1))


def _nsa_sel_decode(layer, page_table, sel, nq, ngate, ocmp, new_rows, win_kt, win_vt, egate, cache_kt, cache_vt,
                    n_sel):
    B, n_pages = page_table.shape
    nw = win_kt.shape[3]
    halves = PAGE_SIZE // BLK_SLC
    per_seq = lambda b, pt, sl: (b, 0, 0)
    const = lambda b, pt, sl: (0, 0)

    def blk_spec(n):
        def index_map(b, pt, sl):
            bb = jnp.minimum(b, B - 1)
            blk = jnp.clip(sl[bb * (NSA_KV_HEADS * n_sel) + n], 0, n_pages * halves - 1)
            return (layer, pt[bb * n_pages + blk // halves], 0, 0)
        return pl.BlockSpec((None, None, NSA_KV, PAGE_SIZE), index_map)
    blk_specs = [blk_spec(n) for n in range(NSA_KV_HEADS * n_sel)]
    win_spec = pl.BlockSpec((1, 1, NSA_KV, nw), lambda b, pt, sl: (layer, b, 0, 0))
    new_t = [jnp.transpose(new_rows[2]), jnp.transpose(new_rows[3])]
    in_specs = ([pl.BlockSpec((1, 1, NSA_Q), per_seq), pl.BlockSpec((1, 1, 3 * NSA_HEADS), per_seq),
                 pl.BlockSpec((1, NSA_KV_HEADS, 8, NSA_KV), lambda b, pt, sl: (b, 0, 0, 0))]
                + [pl.BlockSpec((1, 1, NSA_KV), per_seq)] * 4
                + [pl.BlockSpec((NSA_KV, B), const)] * 2
                + [win_spec, win_spec, pl.BlockSpec(egate.shape, const)]
                + blk_specs + blk_specs)
    grid_spec = pltpu.PrefetchScalarGridSpec(
        num_scalar_prefetch=2, grid=(B,), in_specs=in_specs,
        out_specs=[pl.BlockSpec((1, 1, NSA_Q), per_seq), pl.BlockSpec((1, NSA_KV, nw), per_seq),
                   pl.BlockSpec((1, NSA_KV, nw), per_seq)])
    out, wko, wvo = pl.pallas_call(
        functools.partial(_nsa_sel_decode_kernel, n_sel=n_sel), grid_spec=grid_spec,
        out_shape=[jax.ShapeDtypeStruct((B, 1, NSA_Q), BF16), jax.ShapeDtypeStruct((B, NSA_KV, nw), F32),
                   jax.ShapeDtypeStruct((B, NSA_KV, nw), F32)],
        compiler_params=_cparams(1), name="nsa_sel_decode")(
            page_table.reshape(-1), sel.reshape(-1), nq.reshape(B, 1, -1), ngate.reshape(B, 1, -1), ocmp,
            *[r.reshape(B, 1, -1) for r in new_rows], *new_t, win_kt, win_vt, egate,
            *([cache_kt] * (NSA_KV_HEADS * n_sel)), *([cache_vt] * (NSA_KV_HEADS * n_sel)))
    return out.reshape(B, -1), wko, wvo


def _cumsum_kernel(x_ref, o_ref):
    x = x_ref[...]
    n = x.shape[-1]
    lane = _iota(x.shape, 1)
    shift = 1
    while shift < n:
        x = x + jnp.where(lane >= shift, pltpu.roll(x, shift, axis=1), 0.0)
        shift *= 2
    o_ref[...] = x


def _cumsum_lanes(x):
    return pl.pallas_call(_cumsum_kernel, out_shape=jax.ShapeDtypeStruct(x.shape, F32), name="logf_cumsum")(x)


def _rot_cols(w, d):
    w3 = w.reshape(w.shape[0], -1, d)
    return jnp.concatenate([-w3[..., d // 2:], w3[..., :d // 2]], axis=-1).reshape(w.shape)


def _slot_cols(w):
    n = w.shape[0]
    return w.reshape(n, NSA_KV_HEADS, NSA_GROUP, HEAD_DIM).transpose(0, 2, 1, 3).reshape(n, NSA_Q)


def _layer_weights(w_in, b_forget, g_mla_q, g_mla_kv, w_mla_uq, w_mla_uk, w_mla_uv, phi_k, phi_v,
                   w_br_mla, w_br_nsa, w_br_fox, w_out, g_mix, g_ffn, w_ff1, w_ff2):
    o = 0
    cols = {}
    for name, n in (('cq', MLA_Q_LORA), ('ckv', MLA_KV_LORA), ('kr', MLA_ROPE), ('nq', NSA_Q), ('nkv', 6 * NSA_KV),
                    ('ng', 3 * NSA_HEADS), ('fqkv', 3 * FOX_D), ('ff', FOX_HEADS), ('mg', N_BRANCH * D_MODEL)):
        cols[name] = w_in[:, o:o + n]
        o += n
    pad = jnp.zeros((D_MODEL, 128 - 3 * NSA_HEADS - FOX_HEADS), F32)
    wsm = jnp.concatenate([cols['ng'], cols['ff'], pad], axis=1)
    wkr = jnp.concatenate([jnp.tile(cols['kr'], (1, MLA_HEADS)),
                           jnp.tile(_rot_cols(cols['kr'], MLA_ROPE), (1, MLA_HEADS))], axis=1)
    nq = _slot_cols(cols['nq'])
    nkv = cols['nkv']
    nk = jnp.concatenate([nkv[:, i * NSA_KV:(i + 1) * NSA_KV] for i in (0, 2, 4)], axis=1)
    uq = w_mla_uq.reshape(MLA_Q_LORA, MLA_HEADS, MLA_NOPE + MLA_ROPE)
    uq_nope = uq[:, :, :MLA_NOPE].reshape(MLA_Q_LORA, -1)
    uq_rope = uq[:, :, MLA_NOPE:].reshape(MLA_Q_LORA, -1)
    eye_h = jnp.eye(MLA_HEADS, dtype=F32)
    wuk = jnp.einsum('chn,hk->hnkc', w_mla_uk, eye_h).reshape(MLA_HEADS * MLA_NOPE, MLA_HEADS * MLA_KV_LORA)
    wuv = jnp.einsum('chv,hk->hckv', w_mla_uv, eye_h).reshape(MLA_HEADS, MLA_KV_LORA, MLA_HEADS * MLA_V)
    eye_g = jnp.eye(NSA_KV_HEADS, dtype=F32)

    def phi_big(phi):
        p3 = phi.reshape(BLK_CMP, HEAD_DIM, HEAD_DIM)
        return jnp.einsum('rde,gk->rgdke', p3, eye_g).reshape(CMP_FLAT, NSA_KV).astype(BF16)
    bias = b_forget.reshape(1, FOX_HEADS)
    proj = dict(
        gmix=g_mix.reshape(1, -1), wa=jnp.concatenate([cols['cq'], cols['ckv']], axis=1).astype(BF16),
        wkr=wkr.astype(BF16), wsm=wsm.astype(BF16), wnq=jnp.concatenate([nq, _rot_cols(nq, HEAD_DIM)], axis=1).astype(BF16),
        wnkv=jnp.concatenate([nkv, _rot_cols(nk, HEAD_DIM)], axis=1).astype(BF16),
        wfox=cols['fqkv'].astype(BF16), gq=g_mla_q.reshape(1, -1), gkv=g_mla_kv.reshape(1, -1),
        wuq=jnp.concatenate([uq_nope, uq_rope, _rot_cols(uq_rope, MLA_ROPE)], axis=1).astype(BF16),
        wuk=wuk.astype(BF16), bias=bias)
    merge = dict(
        gmix=g_mix.reshape(1, -1), wmg=cols['mg'].astype(BF16), wbm=w_br_mla.astype(BF16),
        wbn=_slot_cols(w_br_nsa.T).T.astype(BF16), wbf=w_br_fox.astype(BF16), wout=w_out.astype(BF16))
    ffn = dict(g=g_ffn.reshape(1, -1), w1=w_ff1.astype(BF16), w2=w_ff2.astype(BF16))
    return dict(proj=proj, merge=merge, ffn=ffn, wuv=wuv.astype(BF16), phik=phi_big(phi_k), phiv=phi_big(phi_v))


def _rope_table(pos):
    def cs(d, reps):
        inv = ROPE_THETA ** (-jnp.arange(0, d, 2, dtype=F32) / d)
        ang = pos.astype(F32)[:, None] * inv[None, :]
        c, s = jnp.cos(ang), jnp.sin(ang)
        return jnp.tile(jnp.concatenate([c, c], axis=1), (1, reps)), jnp.tile(jnp.concatenate([s, s], axis=1), (1, reps))
    c64, s64 = cs(HEAD_DIM, NSA_HEADS)
    c32, s32 = cs(MLA_ROPE, MLA_HEADS)
    return jnp.concatenate([c64, s64, c32, s32], axis=1)


def _pair_permutation():
    per_page = PAGE_SIZE // BLK_CMP
    row = jnp.arange(2 * PAGE_SIZE)
    r, page, blk = row // (2 * per_page), (row // per_page) % 2, row % per_page
    src = page * PAGE_SIZE + blk * BLK_CMP + r
    return (src[:, None] == jnp.arange(2 * PAGE_SIZE)[None, :]).astype(BF16)


def _selection_constants(S):
    t = ATT_TILE
    gate_col = jnp.arange(3 * NSA_HEADS)
    lane = jnp.arange(3 * NSA_Q)
    br, rem = lane // NSA_Q, lane % NSA_Q
    head = (rem % NSA_KV) // HEAD_DIM * NSA_GROUP + rem // NSA_KV
    egate = (gate_col[:, None] == (head * 3 + br)[None, :]).astype(BF16)
    out = dict(egate=egate)
    if S is not None:
        n_cmp, n_slc = S // BLK_CMP, S // BLK_SLC
        out['pair'] = (jnp.arange(n_cmp)[:, None] // (BLK_SLC // BLK_CMP) == jnp.arange(n_slc)[None, :]).astype(BF16)
        key_blk = jnp.arange(S) // BLK_SLC
        eblk = (jnp.arange(n_slc)[:, None] == key_blk[None, :]).astype(BF16)
        out['eblk'] = eblk.reshape(n_slc, S // t, t).transpose(1, 0, 2)
    return out


def _token_tile(T):
    for tm in (512, 256, 128):
        if T % tm == 0:
            return tm
    return T


def _prompt_mixer(x, lw, tab, consts, B, S):
    tm = min(_token_tile(B * S), S)
    pr = _project(x, tab, S, lw['proj'], tm)
    o_mla = _mla_prompt(pr['qmla'], pr['kcat'], lw['wuv'], B, S)
    t = ATT_TILE
    logf_t = pr['fox_logf'].reshape(B, S, FOX_HEADS).transpose(0, 2, 1).reshape(B * FOX_HEADS, S)
    cum = _cumsum_lanes(logf_t).reshape(B, FOX_HEADS, S)
    cum_q = cum.transpose(0, 2, 1).reshape(B * S, FOX_HEADS)
    cum_k = cum.reshape(B, FOX_HEADS, S // t, t).transpose(0, 2, 1, 3)
    o_fox = _fox_prompt(pr['fq'], pr['fox_k'], pr['fox_v'], cum_q, cum_k, B, S)
    n_blk = B * S // BLK_CMP
    kc, vc = _compress(pr['nsa_cmp_k'].reshape(n_blk, CMP_FLAT), pr['nsa_cmp_v'].reshape(n_blk, CMP_FLAT),
                       lw['phik'], lw['phiv'])
    o_nsa = _nsa_prompt(pr['nq'], pr['ngate'], kc, vc, pr['nsa_slc_k'], pr['nsa_slc_v'],
                        pr['nsa_win_k'], pr['nsa_win_v'], consts, B, S)
    x1 = _merge(x, o_mla, o_nsa, o_fox, lw['merge'], tm)
    return x1, pr


def _sample_mixer(x, lw, tab, consts, layer, page_table, caches):
    B = x.shape[0]
    n_pages = page_table.shape[1]
    pr = _project(x, tab, B, lw['proj'], B)
    P = min(16, n_pages)
    o_mla = _mla_decode(layer, page_table, pr['qmla'], pr['mla_latent'], pr['mla_krope'], lw['wuv'],
                        caches['mla_latent'], caches['mla_krope'], P)
    o_fox = _fox_decode(layer, page_table, pr['fq'], pr['fox_k'], pr['fox_v'], pr['fox_logf'], consts['tri'],
                        caches['fox_k'], caches['fox_v'], caches['fox_logf_t'], min(8, n_pages))
    n_sel = TOP_N - 1
    ocmp, sel = _nsa_cmp_decode(layer, page_table, pr['nq'], lw['phik'], lw['phiv'], consts['pair_dec'],
                                consts['perm'], caches['nsa_cmp_k'], caches['nsa_cmp_v'], P, n_sel)
    sel = sel[:, :, 0, :n_sel]
    new_rows = (pr['nsa_slc_k'], pr['nsa_slc_v'], pr['nsa_win_k'], pr['nsa_win_v'])
    o_nsa, wko, wvo = _nsa_sel_decode(layer, page_table, sel, pr['nq'], pr['ngate'], ocmp, new_rows,
                                      caches['nsa_win_k'], caches['nsa_win_v'], consts['egate'],
                                      caches['nsa_slc_k'], caches['nsa_slc_v'], n_sel)
    x1 = _merge(x, o_mla, o_nsa, o_fox, lw['merge'], B)
    pr = dict(pr, nsa_win_k=wko, nsa_win_v=wvo)
    return x1, pr


STATE_NAMES = ('mla_latent', 'mla_krope', 'nsa_cmp_k', 'nsa_cmp_v', 'nsa_slc_k', 'nsa_slc_v', 'nsa_win_k', 'nsa_win_v',
               'fox_k', 'fox_v', 'fox_logf')


def kernel(x_prompt, x_sample, cache_mla_latent, cache_mla_krope, cache_nsa_cmp_k, cache_nsa_cmp_v, cache_nsa_slc_k, cache_nsa_slc_v, state_nsa_win_k, state_nsa_win_v, cache_fox_k, cache_fox_v, cache_fox_logf, page_table, w_in, b_forget, g_mla_q, g_mla_kv, w_mla_uq, w_mla_uk, w_mla_uv, w_nsa_phi_k, w_nsa_phi_v, w_br_mla, w_br_nsa, w_br_fox, w_out, g_norm_mix, g_norm_ffn, w_ff1, w_ff2, g_norm_final):
    B, S, _ = x_prompt.shape
    Bs, Ss, _ = x_sample.shape
    depth = w_in.shape[0]
    n_pages = page_table.shape[1]
    past = n_pages * PAGE_SIZE
    nw = state_nsa_win_k.shape[2]
    assert Ss == 1 and nw == WINDOW and S % ATT_TILE == 0 and S >= WINDOW
    assert past // BLK_SLC >= TOP_N and past // BLK_CMP <= 2 * 128

    def keys_on_lanes(c):
        return jnp.transpose(c, (0, 1, 3, 4, 2)).reshape(c.shape[0], c.shape[1], c.shape[3] * c.shape[4], c.shape[2])
    caches = {
        'mla_latent': cache_mla_latent, 'mla_krope': jnp.swapaxes(cache_mla_krope, 2, 3),
        'nsa_cmp_k': keys_on_lanes(cache_nsa_cmp_k), 'nsa_cmp_v': keys_on_lanes(cache_nsa_cmp_v),
        'nsa_slc_k': keys_on_lanes(cache_nsa_slc_k), 'nsa_slc_v': keys_on_lanes(cache_nsa_slc_v),
        'nsa_win_k': keys_on_lanes(state_nsa_win_k), 'nsa_win_v': keys_on_lanes(state_nsa_win_v),
        'fox_k': keys_on_lanes(cache_fox_k), 'fox_v': keys_on_lanes(cache_fox_v),
        'fox_logf_t': jnp.swapaxes(cache_fox_logf, 2, 3),
    }
    tab_p = _rope_table(jnp.arange(S, dtype=jnp.int32))
    tab_s = _rope_table(jnp.full((Bs,), past, dtype=jnp.int32))
    consts_p = _selection_constants(S)
    consts_s = dict(
        egate=consts_p['egate'],
        pair_dec=(jnp.arange(past // BLK_CMP)[:, None] // (BLK_SLC // BLK_CMP)
                  == jnp.arange(past // BLK_SLC)[None, :]).astype(BF16),
        tri=(jnp.arange(PAGE_SIZE)[:, None] > jnp.arange(PAGE_SIZE)[None, :]).astype(BF16),
        perm=_pair_permutation())

    xp = x_prompt.reshape(B * S, D_MODEL)
    xs = x_sample.reshape(Bs, D_MODEL)
    gfin = g_norm_final.reshape(1, -1)
    rows_p = {n: [] for n in STATE_NAMES}
    rows_s = {n: [] for n in STATE_NAMES}
    yp = ys = None
    for layer in range(depth):
        lw = _layer_weights(w_in[layer], b_forget[layer], g_mla_q[layer], g_mla_kv[layer], w_mla_uq[layer],
                            w_mla_uk[layer], w_mla_uv[layer], w_nsa_phi_k[layer], w_nsa_phi_v[layer],
                            w_br_mla[layer], w_br_nsa[layer], w_br_fox[layer], w_out[layer],
                            g_norm_mix[layer], g_norm_ffn[layer], w_ff1[layer], w_ff2[layer])
        xp, pr_p = _prompt_mixer(xp, lw, tab_p, consts_p, B, S)
        xp, yp = _ffn(xp, lw['ffn']['g'], lw['ffn']['w1'], lw['ffn']['w2'], gfin, _token_tile(B * S))
        xs, pr_s = _sample_mixer(xs, lw, tab_s, consts_s, layer, page_table, caches)
        xs, ys = _ffn(xs, lw['ffn']['g'], lw['ffn']['w1'], lw['ffn']['w2'], gfin, Bs)
        for n in STATE_NAMES:
            rows_p[n].append(pr_p[n])
            rows_s[n].append(pr_s[n])

    def stack_p(n):
        a = jnp.stack(rows_p[n]).reshape(depth, B, S, -1)
        if n in ('nsa_win_k', 'nsa_win_v'):
            a = a[:, :, S - min(WINDOW, S):]
        if n.startswith('nsa_'):
            return a.reshape(a.shape[:3] + (NSA_KV_HEADS, HEAD_DIM))
        if n in ('fox_k', 'fox_v'):
            return a.reshape(a.shape[:3] + (FOX_HEADS, HEAD_DIM))
        return a

    def stack_s(n):
        a = jnp.stack(rows_s[n])
        if n in ('nsa_win_k', 'nsa_win_v'):
            return jnp.transpose(a.reshape(depth, Bs, NSA_KV_HEADS, HEAD_DIM, nw), (0, 1, 4, 2, 3))
        a = a.reshape(depth, Bs, 1, -1)
        if n.startswith('nsa_'):
            return a.reshape(a.shape[:3] + (NSA_KV_HEADS, HEAD_DIM))
        if n in ('fox_k', 'fox_v'):
            return a.reshape(a.shape[:3] + (FOX_HEADS, HEAD_DIM))
        return a

    outs = [yp.reshape(B, S, D_MODEL), ys.reshape(Bs, 1, D_MODEL)]
    for n in STATE_NAMES:
        outs += [stack_p(n), stack_s(n)]
    return tuple(outs)
```

```python
import functools

import jax
import jax.numpy as jnp
from jax import lax
from jax.experimental import pallas as pl
from jax.experimental.pallas import tpu as pltpu

D_MODEL = 1024
PAGE_SIZE = 128
HEAD_DIM = 64
MLA_HEADS = 4
MLA_NOPE = 64
MLA_ROPE = 32
MLA_V = 64
MLA_Q_LORA = 384
MLA_KV_LORA = 256
NSA_HEADS = 8
NSA_KV_HEADS = 2
NSA_GROUP = NSA_HEADS // NSA_KV_HEADS
BLK_CMP = 32
BLK_SLC = 64
TOP_N = 8
WINDOW = 512
FOX_HEADS = 4
D_FF = 4 * D_MODEL
ROPE_THETA = 10000.0
EPS = 1e-6
N_BRANCH = 3
NEG_INF = -1e30
FORCE_SCORE = 1e4

NSA_Q = NSA_HEADS * HEAD_DIM
NSA_KV = NSA_KV_HEADS * HEAD_DIM
FOX_D = FOX_HEADS * HEAD_DIM
MLA_QCAT = MLA_KV_LORA + MLA_HEADS * MLA_ROPE
CMP_FLAT = BLK_CMP * NSA_KV
TAB_W = 2 * NSA_Q + 2 * MLA_HEADS * MLA_ROPE

VMEM_LIMIT_BYTES = 56 * 1024 * 1024
ATT_TILE = 512
MLA_PAGES_PER_STEP = 64
FOX_PAGES_PER_STEP = 32
CMP_PAGES_PER_STEP = 64
F32 = jnp.float32
BF16 = jnp.bfloat16


def _cparams(n_axes):
    return pltpu.CompilerParams(dimension_semantics=("arbitrary",) * n_axes,
                                vmem_limit_bytes=VMEM_LIMIT_BYTES)


def _rms(x, g):
    return x * lax.rsqrt(jnp.mean(x * x, axis=-1, keepdims=True) + EPS) * g


def _dot(a, b):
    return jnp.dot(a, b, preferred_element_type=F32)


def _dot_nt(a, b):
    return lax.dot_general(a, b, (((1,), (1,)), ((), ())), preferred_element_type=F32)


def _split3(x):
    hi = x.astype(BF16)
    r = x - hi.astype(F32)
    mid = r.astype(BF16)
    lo = (r - mid.astype(F32)).astype(BF16)
    return hi, mid, lo


def _dot_sel(x, e):
    hi, mid, lo = _split3(x)
    return _dot(hi, e) + _dot(mid, e) + _dot(lo, e)


def _iota(shape, dim):
    return lax.broadcasted_iota(jnp.int32, shape, dim)


def _proj_kernel(x_ref, gmix_ref, wa_ref, wkr_ref, wsm_ref, wnq_ref, wnkv_ref, wfox_ref, gq_ref, gkv_ref,
                 wuq_ref, wuk_ref, bias_ref, tab_ref,
                 qmla_ref, kcat_ref, lat_ref, krope_ref, nq_ref,
                 cmpk_ref, cmpv_ref, slck_ref, slcv_ref, wink_ref, winv_ref,
                 ngate_ref, fq_ref, fk_ref, fv_ref, logf_ref):
    x = x_ref[...]
    hb = _rms(x, gmix_ref[...]).astype(BF16)
    tab = tab_ref[...]
    cos64, sin64 = tab[:, :NSA_Q], tab[:, NSA_Q:2 * NSA_Q]
    cos32, sin32 = tab[:, 2 * NSA_Q:2 * NSA_Q + 128], tab[:, 2 * NSA_Q + 128:]

    za = _dot(hb, wa_ref[...])
    cqn = _rms(za[:, :MLA_Q_LORA], gq_ref[...]).astype(BF16)
    zq = _dot(cqn, wuq_ref[...])
    qlat = _dot(zq[:, :256].astype(BF16), wuk_ref[...])
    qrope = zq[:, 256:384] * cos32 + zq[:, 384:512] * sin32
    lane = _iota(qrope.shape, 1)
    for h in range(MLA_HEADS):
        qmla_ref[:, h * MLA_QCAT:h * MLA_QCAT + 256] = qlat[:, h * 256:(h + 1) * 256].astype(BF16)
        own = (lane >= h * MLA_ROPE) & (lane < (h + 1) * MLA_ROPE)
        qmla_ref[:, h * MLA_QCAT + 256:(h + 1) * MLA_QCAT] = jnp.where(own, qrope, 0.0).astype(BF16)
    lat = _rms(za[:, MLA_Q_LORA:], gkv_ref[...])
    lat_ref[...] = lat
    zkr = _dot(hb, wkr_ref[...])
    kr4 = zkr[:, :128] * cos32 + zkr[:, 128:] * sin32
    krope_ref[...] = kr4[:, :MLA_ROPE]
    kcat_ref[:, :256] = lat.astype(BF16)
    kcat_ref[:, 256:] = kr4.astype(BF16)
    zs = _dot(hb, wsm_ref[...])
    ngate_ref[...] = jax.nn.sigmoid(zs[:, 0:24])
    logf_ref[...] = jax.nn.log_sigmoid(zs[:, 24:28] + bias_ref[...])

    zn = _dot(hb, wnq_ref[...])
    nq_ref[...] = ((zn[:, :NSA_Q] * cos64 + zn[:, NSA_Q:] * sin64) * (HEAD_DIM ** -0.5)).astype(BF16)
    zk = _dot(hb, wnkv_ref[...])
    c2, s2 = cos64[:, :NSA_KV], sin64[:, :NSA_KV]
    for i, ref in enumerate((cmpk_ref, cmpv_ref, slck_ref, slcv_ref, wink_ref, winv_ref)):
        z = zk[:, i * NSA_KV:(i + 1) * NSA_KV]
        if i % 2 == 0:
            zr = zk[:, (6 + i // 2) * NSA_KV:(7 + i // 2) * NSA_KV]
            z = z * c2 + zr * s2
        ref[...] = z

    zf = _dot(hb, wfox_ref[...])
    fq_ref[...] = (zf[:, :FOX_D] * (HEAD_DIM ** -0.5)).astype(BF16)
    fk_ref[...] = zf[:, FOX_D:2 * FOX_D]
    fv_ref[...] = zf[:, 2 * FOX_D:]


def _project(x, tab, tab_rows_per_block, pw, tm):
    T = x.shape[0]
    nt = T // tm
    n_tab_blocks = tab.shape[0] // tm
    row = lambda i: (i, 0)
    const = lambda i: (0, 0)
    tab_map = (lambda i: (i % n_tab_blocks, 0)) if n_tab_blocks > 1 else const
    weights = [pw['gmix'], pw['wa'], pw['wkr'], pw['wsm'], pw['wnq'], pw['wnkv'], pw['wfox'], pw['gq'], pw['gkv'],
               pw['wuq'], pw['wuk'], pw['bias']]
    in_specs = ([pl.BlockSpec((tm, D_MODEL), row)]
                + [pl.BlockSpec(w.shape, const) for w in weights]
                + [pl.BlockSpec((tm, TAB_W), tab_map)])
    widths = [(MLA_HEADS * MLA_QCAT, BF16), (MLA_QCAT, BF16), (MLA_KV_LORA, F32), (MLA_ROPE, F32), (NSA_Q, BF16)]
    widths += [(NSA_KV, F32)] * 6
    widths += [(3 * NSA_HEADS, F32), (FOX_D, BF16), (FOX_D, F32), (FOX_D, F32), (FOX_HEADS, F32)]
    out_shape = [jax.ShapeDtypeStruct((T, w), dt) for w, dt in widths]
    out_specs = [pl.BlockSpec((tm, w), row) for w, _ in widths]
    outs = pl.pallas_call(
        _proj_kernel, grid=(nt,), in_specs=in_specs, out_specs=out_specs, out_shape=out_shape,
        compiler_params=_cparams(1), name="proj")(x, *weights, tab)
    names = ('qmla', 'kcat', 'mla_latent', 'mla_krope', 'nq', 'nsa_cmp_k', 'nsa_cmp_v', 'nsa_slc_k', 'nsa_slc_v',
             'nsa_win_k', 'nsa_win_v', 'ngate', 'fq', 'fox_k', 'fox_v', 'fox_logf')
    return dict(zip(names, outs))


def _flash_update(s, v, m_ref, l_ref, acc_ref, mask=None):
    m_prev = m_ref[...]
    m_new = jnp.maximum(m_prev, jnp.max(s, axis=-1, keepdims=True))
    alpha = jnp.exp(m_prev - m_new)
    p = jnp.exp(s - m_new)
    if mask is not None:
        p = jnp.where(mask, p, 0.0)
    l_ref[...] = alpha * l_ref[...] + jnp.sum(p, axis=-1, keepdims=True)
    acc_ref[...] = alpha * acc_ref[...] + _dot(p.astype(BF16), v)
    m_ref[...] = m_new


def _flash_init(m_ref, l_ref, acc_ref):
    m_ref[...] = jnp.full(m_ref.shape, NEG_INF, F32)
    l_ref[...] = jnp.zeros(l_ref.shape, F32)
    acc_ref[...] = jnp.zeros(acc_ref.shape, F32)


def _mla_prompt_kernel(q_ref, kcat_ref, wuv_ref, o_ref, m_ref, l_ref, acc_ref, *, t, scale):
    i = pl.program_id(1)
    q = jnp.concatenate([q_ref[:, h * MLA_QCAT:(h + 1) * MLA_QCAT] for h in range(MLA_HEADS)], axis=0)
    _flash_init(m_ref, l_ref, acc_ref)

    def step(j, causal):
        k = kcat_ref[pl.ds(pl.multiple_of(j * t, t), t), :]
        s = _dot_nt(q, k) * scale
        if causal:
            qpos = _iota(s.shape, 0) & (t - 1)
            s = jnp.where(_iota(s.shape, 1) <= qpos, s, NEG_INF)
        _flash_update(s, k[:, :MLA_KV_LORA], m_ref, l_ref, acc_ref)

    def body(j, c):
        step(j, False)
        return c
    lax.fori_loop(0, i, body, 0)
    step(i, True)
    o_lat = (acc_ref[...] / l_ref[...]).astype(BF16)
    out = _dot(o_lat[0:t], wuv_ref[0])
    for h in range(1, MLA_HEADS):
        out = out + _dot(o_lat[h * t:(h + 1) * t], wuv_ref[h])
    o_ref[...] = out.astype(BF16)


def _mla_prompt(qmla, kcat, wuv, B, S):
    t = ATT_TILE
    nq = S // t
    kern = functools.partial(_mla_prompt_kernel, t=t, scale=(MLA_NOPE + MLA_ROPE) ** -0.5)
    return pl.pallas_call(
        kern, grid=(B, nq),
        in_specs=[pl.BlockSpec((t, MLA_HEADS * MLA_QCAT), lambda b, i: (b * nq + i, 0)),
                  pl.BlockSpec((S, MLA_QCAT), lambda b, i: (b, 0)),
                  pl.BlockSpec(wuv.shape, lambda b, i: (0, 0, 0))],
        out_specs=pl.BlockSpec((t, MLA_HEADS * MLA_V), lambda b, i: (b * nq + i, 0)),
        out_shape=jax.ShapeDtypeStruct((B * S, MLA_HEADS * MLA_V), BF16),
        scratch_shapes=[pltpu.VMEM((MLA_HEADS * t, 1), F32), pltpu.VMEM((MLA_HEADS * t, 1), F32),
                        pltpu.VMEM((MLA_HEADS * t, MLA_KV_LORA), F32)],
        compiler_params=_cparams(2), name="mla_prompt")(qmla, kcat, wuv)


def _fox_prompt_kernel(q_ref, k_ref, v_ref, cq_ref, ck_ref, o_ref, m_ref, l_ref, acc_ref, *, t):
    i = pl.program_id(1)
    q = q_ref[...]
    cq = cq_ref[...]
    head_of_lane = lax.shift_right_logical(_iota(q.shape, 1), 6)
    qh = [jnp.where(head_of_lane == h, q, jnp.zeros_like(q)) for h in range(FOX_HEADS)]
    m_ref[...] = jnp.full(m_ref.shape, NEG_INF, F32)
    l_ref[...] = jnp.zeros(l_ref.shape, F32)
    acc_ref[...] = jnp.zeros(acc_ref.shape, F32)

    def step(j, causal):
        rows = pl.ds(pl.multiple_of(j * t, t), t)
        k = k_ref[rows, :].astype(BF16)
        v = v_ref[rows, :].astype(BF16)
        ck = ck_ref[0, j]
        for h in range(FOX_HEADS):
            s = _dot_nt(qh[h], k) + (cq[:, h:h + 1] - ck[h:h + 1, :])
            if causal:
                s = jnp.where(_iota(s.shape, 1) <= _iota(s.shape, 0), s, NEG_INF)
            _flash_update(s, v, m_ref.at[h], l_ref.at[h], acc_ref.at[h])

    def body(j, c):
        step(j, False)
        return c
    lax.fori_loop(0, i, body, 0)
    step(i, True)
    out = jnp.zeros(q.shape, F32)
    for h in range(FOX_HEADS):
        out = jnp.where(head_of_lane == h, acc_ref[h] / l_ref[h], out)
    o_ref[...] = out.astype(BF16)


def _fox_prompt(fq, fk, fv, cum_q, cum_k, B, S):
    t = ATT_TILE
    nq = S // t
    tile = lambda b, i: (b * nq + i, 0)
    seq = lambda b, i: (b, 0)
    return pl.pallas_call(
        functools.partial(_fox_prompt_kernel, t=t), grid=(B, nq),
        in_specs=[pl.BlockSpec((t, FOX_D), tile), pl.BlockSpec((S, FOX_D), seq), pl.BlockSpec((S, FOX_D), seq),
                  pl.BlockSpec((t, FOX_HEADS), tile),
                  pl.BlockSpec((1, nq, FOX_HEADS, t), lambda b, i: (b, 0, 0, 0))],
        out_specs=pl.BlockSpec((t, FOX_D), tile),
        out_shape=jax.ShapeDtypeStruct((B * S, FOX_D), BF16),
        scratch_shapes=[pltpu.VMEM((FOX_HEADS, t, 1), F32), pltpu.VMEM((FOX_HEADS, t, 1), F32),
                        pltpu.VMEM((FOX_HEADS, t, FOX_D), F32)],
        compiler_params=_cparams(2), name="fox_prompt")(fq, fk, fv, cum_q, cum_k)


def _compress_kernel(xk_ref, xv_ref, pk_ref, pv_ref, kc_ref, vc_ref):
    kc_ref[...] = _dot(xk_ref[...].astype(BF16), pk_ref[...])
    vc_ref[...] = _dot(xv_ref[...].astype(BF16), pv_ref[...])


def _compress(xk, xv, phik, phiv):
    n = xk.shape[0]
    tr = min(n, 128)
    row = lambda i: (i, 0)
    const = lambda i: (0, 0)
    return pl.pallas_call(
        _compress_kernel, grid=(n // tr,),
        in_specs=[pl.BlockSpec((tr, CMP_FLAT), row), pl.BlockSpec((tr, CMP_FLAT), row),
                  pl.BlockSpec(phik.shape, const), pl.BlockSpec(phiv.shape, const)],
        out_specs=[pl.BlockSpec((tr, NSA_KV), row)] * 2,
        out_shape=[jax.ShapeDtypeStruct((n, NSA_KV), F32)] * 2,
        compiler_params=_cparams(1), name="nsa_compress")(xk, xv, phik, phiv)


def _topk_mask(vals, k):
    n = vals.shape[-1]
    lane = _iota(vals.shape, 1).astype(F32)
    sel = jnp.zeros(vals.shape, F32)
    for _ in range(k):
        m = jnp.max(vals, axis=-1, keepdims=True)
        first = jnp.min(jnp.where(vals == m, lane, float(n)), axis=-1, keepdims=True)
        hit = lane == first
        sel = jnp.where(hit, 1.0, sel)
        vals = jnp.where(hit, -2.0, vals)
    return sel


def _slots_from_groups(o0, o1, t):
    low = _iota((t, NSA_KV), 1) < HEAD_DIM
    return jnp.concatenate([jnp.where(low, o0[j * t:(j + 1) * t], o1[j * t:(j + 1) * t])
                            for j in range(NSA_GROUP)], axis=1)


def _nsa_prompt_kernel(q_ref, gate_ref, kc_ref, vc_ref, sk_ref, sv_ref, wk_ref, wv_ref,
                       egate_ref, pair_ref, eblk_ref, o_ref, m_ref, l_ref, acc_ref, *, t, n_cmp, n_slc):
    i = pl.program_id(1)
    q = q_ref[...]
    zero = jnp.zeros((t, NSA_KV), BF16)
    half = lax.shift_right_logical(_iota((t, NSA_KV), 1), 6)
    rows4 = (NSA_GROUP * t, 1)
    qpos4 = i * t + (_iota(rows4, 0) & (t - 1))
    qpos = i * t + _iota((t, 1), 0)
    kc = kc_ref[...].astype(BF16)
    vc = vc_ref[...].astype(BF16)
    outs = [[], [], []]
    for g in range(NSA_KV_HEADS):
        qg = jnp.concatenate([jnp.where(half == g, q[:, j * NSA_KV:(j + 1) * NSA_KV], zero)
                              for j in range(NSA_GROUP)], axis=0)
        s = _dot_nt(qg, kc)
        cmp_end = (_iota(s.shape, 1) + 1) * BLK_CMP - 1
        vis = cmp_end <= qpos4
        s = jnp.where(vis, s, NEG_INF)
        p = jnp.where(vis, jnp.exp(s - jnp.max(s, axis=-1, keepdims=True)), 0.0)
        pc = p / jnp.maximum(jnp.sum(p, axis=-1, keepdims=True), 1e-30)
        outs[0].append(_dot(pc.astype(BF16), vc))
        imp = pc[0:t] + pc[t:2 * t] + pc[2 * t:3 * t] + pc[3 * t:4 * t]
        imp = _dot_sel(imp, pair_ref[...])
        blk = _iota(imp.shape, 1)
        cur = lax.shift_right_logical(qpos, 6)
        forced = (blk == 0) | (blk == cur) | (blk == cur - 1)
        imp = jnp.where(forced, FORCE_SCORE, jnp.where(blk > cur, -1.0, imp))
        sel = _topk_mask(imp, min(TOP_N, n_slc)).astype(BF16)
        sel4 = jnp.concatenate([sel] * NSA_GROUP, axis=0)

        _flash_init(m_ref, l_ref, acc_ref)

        def slc_step(j, c):
            rows = pl.ds(pl.multiple_of(j * t, t), t)
            k = sk_ref[rows, :].astype(BF16)
            v = sv_ref[rows, :].astype(BF16)
            s = _dot_nt(qg, k)
            kpos = j * t + _iota(s.shape, 1)
            mask = (_dot(sel4, eblk_ref[j]) > 0.5) & (kpos <= qpos4)
            _flash_update(jnp.where(mask, s, NEG_INF), v, m_ref, l_ref, acc_ref, mask)
            return c
        lax.fori_loop(0, i + 1, slc_step, 0)
        outs[1].append(acc_ref[...] / jnp.maximum(l_ref[...], 1e-30))

        _flash_init(m_ref, l_ref, acc_ref)

        def win_step(j, c):
            rows = pl.ds(pl.multiple_of(j * t, t), t)
            k = wk_ref[rows, :].astype(BF16)
            v = wv_ref[rows, :].astype(BF16)
            s = _dot_nt(qg, k)
            diff = qpos4 - (j * t + _iota(s.shape, 1))
            mask = (diff >= 0) & (diff <= WINDOW)
            _flash_update(jnp.where(mask, s, NEG_INF), v, m_ref, l_ref, acc_ref, mask)
            return c
        lax.fori_loop(jnp.maximum(i - WINDOW // t, 0), i + 1, win_step, 0)
        outs[2].append(acc_ref[...] / jnp.maximum(l_ref[...], 1e-30))

    gexp = _dot_sel(gate_ref[...], egate_ref[...])
    out = jnp.zeros((t, NSA_Q), F32)
    for br in range(3):
        out = out + gexp[:, br * NSA_Q:(br + 1) * NSA_Q] * _slots_from_groups(outs[br][0], outs[br][1], t)
    o_ref[...] = out.astype(BF16)


def _nsa_prompt(nq, ngate, kc, vc, sk, sv, wk, wv, consts, B, S):
    t = ATT_TILE
    nt = S // t
    n_cmp = S // BLK_CMP
    n_slc = S // BLK_SLC
    tile = lambda b, i: (b * nt + i, 0)
    seq = lambda b, i: (b, 0)
    c2 = lambda b, i: (0, 0)
    c3 = lambda b, i: (0, 0, 0)
    kern = functools.partial(_nsa_prompt_kernel, t=t, n_cmp=n_cmp, n_slc=n_slc)
    return pl.pallas_call(
        kern, grid=(B, nt),
        in_specs=[pl.BlockSpec((t, NSA_Q), tile), pl.BlockSpec((t, 3 * NSA_HEADS), tile),
                  pl.BlockSpec((n_cmp, NSA_KV), seq), pl.BlockSpec((n_cmp, NSA_KV), seq),
                  pl.BlockSpec((S, NSA_KV), seq), pl.BlockSpec((S, NSA_KV), seq),
                  pl.BlockSpec((S, NSA_KV), seq), pl.BlockSpec((S, NSA_KV), seq),
                  pl.BlockSpec(consts['egate'].shape, c2), pl.BlockSpec(consts['pair'].shape, c2),
                  pl.BlockSpec(consts['eblk'].shape, c3)],
        out_specs=pl.BlockSpec((t, NSA_Q), tile),
        out_shape=jax.ShapeDtypeStruct((B * S, NSA_Q), BF16),
        scratch_shapes=[pltpu.VMEM((NSA_GROUP * t, 1), F32), pltpu.VMEM((NSA_GROUP * t, 1), F32),
                        pltpu.VMEM((NSA_GROUP * t, NSA_KV), F32)],
        compiler_params=_cparams(2), name="nsa_prompt")(
            nq, ngate, kc, vc, sk, sv, wk, wv, consts['egate'], consts['pair'], consts['eblk'])


def _merge_kernel(x_ref, gmix_ref, wmg_ref, omla_ref, onsa_ref, ofox_ref, wbm_ref, wbn_ref, wbf_ref,
                  wout_ref, o_ref):
    x = x_ref[...]
    hb = _rms(x, gmix_ref[...]).astype(BF16)
    acc = None
    for br, (o_r, w_r) in enumerate(((omla_ref, wbm_ref), (onsa_ref, wbn_ref), (ofox_ref, wbf_ref))):
        gate = jax.nn.sigmoid(_dot(hb, wmg_ref[:, br * D_MODEL:(br + 1) * D_MODEL]))
        term = gate * _dot(o_r[...], w_r[...])
        acc = term if acc is None else acc + term
    o_ref[...] = x + _dot(acc.astype(BF16), wout_ref[...])


def _merge(x, o_mla, o_nsa, o_fox, mw, tm):
    T = x.shape[0]
    row = lambda i: (i, 0)
    const = lambda i: (0, 0)
    weights_a = [mw['gmix'], mw['wmg']]
    weights_b = [mw['wbm'], mw['wbn'], mw['wbf'], mw['wout']]
    return pl.pallas_call(
        _merge_kernel, grid=(T // tm,),
        in_specs=([pl.BlockSpec((tm, D_MODEL), row)] + [pl.BlockSpec(w.shape, const) for w in weights_a]
                  + [pl.BlockSpec((tm, o.shape[1]), row) for o in (o_mla, o_nsa, o_fox)]
                  + [pl.BlockSpec(w.shape, const) for w in weights_b]),
        out_specs=pl.BlockSpec((tm, D_MODEL), row),
        out_shape=jax.ShapeDtypeStruct((T, D_MODEL), F32),
        compiler_params=_cparams(1), name="merge")(x, *weights_a, o_mla, o_nsa, o_fox, *weights_b)


FF_CHUNK = 1024


def _ffn_kernel(x_ref, g_ref, w1_ref, w2_ref, gfin_ref, o_ref, y_ref):
    x = x_ref[...]
    hb = _rms(x, g_ref[...]).astype(BF16)
    acc = x
    for c in range(D_FF // FF_CHUNK):
        u = _dot(hb, w1_ref[:, c * FF_CHUNK:(c + 1) * FF_CHUNK])
        u = jnp.square(jnp.maximum(u, 0.0)).astype(BF16)
        acc = acc + _dot(u, w2_ref[c * FF_CHUNK:(c + 1) * FF_CHUNK, :])
    o_ref[...] = acc
    y_ref[...] = _rms(acc, gfin_ref[...])


def _ffn(x, g, w1, w2, gfin, tm):
    T = x.shape[0]
    row = lambda i: (i, 0)
    const = lambda i: (0, 0)
    return pl.pallas_call(
        _ffn_kernel, grid=(T // tm,),
        in_specs=[pl.BlockSpec((tm, D_MODEL), row), pl.BlockSpec(g.shape, const), pl.BlockSpec(w1.shape, const),
                  pl.BlockSpec(w2.shape, const), pl.BlockSpec(gfin.shape, const)],
        out_specs=[pl.BlockSpec((tm, D_MODEL), row)] * 2,
        out_shape=[jax.ShapeDtypeStruct((T, D_MODEL), F32)] * 2,
        compiler_params=_cparams(1), name="ffn")(x, g, w1, w2, gfin)


def _page_specs(block, layer, n_seq, n_pages, pages_per_step, reverse=False):
    def spec(i):
        def index_map(b, c, pt):
            cc = (n_pages // pages_per_step - 1 - c) if reverse else c
            slot = jnp.clip(b * n_pages + cc * pages_per_step + i, 0, n_seq * n_pages - 1)
            return (layer, pt[slot]) + (0,) * (len(block) - 2)
        return pl.BlockSpec(block, index_map)
    return [spec(i) for i in range(pages_per_step)]


def _mla_decode_kernel(pt_ref, q_ref, latn_ref, krn_ref, wuv_ref, *rest, P, scale):
    lat_refs, kr_refs = rest[:P], rest[P:2 * P]
    o_ref, m_ref, l_ref, acc_ref = rest[2 * P:]
    c = pl.program_id(1)

    @pl.when(c == 0)
    def _():
        _flash_init(m_ref, l_ref, acc_ref)

    q = q_ref[0]
    ql = q[:, :MLA_KV_LORA]
    qm = q[:, MLA_KV_LORA:].astype(F32)
    qr = (qm[:, 0:32] + qm[:, 32:64] + qm[:, 64:96] + qm[:, 96:128])
    qrb = qr.astype(BF16)
    ks = [lat_refs[i][...].astype(BF16) for i in range(P)]
    s = jnp.concatenate([_dot_nt(ql, ks[i]) + _dot(qrb, kr_refs[i][...].astype(BF16)) for i in range(P)],
                        axis=1) * scale
    m_prev = m_ref[...]
    m_new = jnp.maximum(m_prev, jnp.max(s, axis=-1, keepdims=True))
    alpha = jnp.exp(m_prev - m_new)
    p = jnp.exp(s - m_new)
    l_ref[...] = alpha * l_ref[...] + jnp.sum(p, axis=-1, keepdims=True)
    pb = p.astype(BF16)
    pv = _dot(pb[:, 0:PAGE_SIZE], ks[0])
    for i in range(1, P):
        pv = pv + _dot(pb[:, i * PAGE_SIZE:(i + 1) * PAGE_SIZE], ks[i])
    acc_ref[...] = alpha * acc_ref[...] + pv
    m_ref[...] = m_new

    @pl.when(c == pl.num_programs(1) - 1)
    def _():
        kn = latn_ref[0]
        s_new = (jnp.sum(ql.astype(F32) * kn, axis=-1, keepdims=True)
                 + jnp.sum(qr * krn_ref[0], axis=-1, keepdims=True)) * scale
        m_prev = m_ref[...]
        m_new = jnp.maximum(m_prev, s_new)
        alpha = jnp.exp(m_prev - m_new)
        p_new = jnp.exp(s_new - m_new)
        l = alpha * l_ref[...] + p_new
        o_lat = ((alpha * acc_ref[...] + p_new * kn) / l).astype(BF16)
        row = _iota((8, MLA_HEADS * MLA_V), 0)
        out = jnp.zeros((8, MLA_HEADS * MLA_V), F32)
        for h in range(MLA_HEADS):
            out = out + jnp.where(row == h, _dot(o_lat, wuv_ref[h]), 0.0)
        o_ref[0] = jnp.sum(out, axis=0, keepdims=True).astype(BF16)


def _mla_decode(layer, page_table, qmla, lat_new, kr_new, wuv, cache_lat, cache_kr, P):
    B, n_pages = page_table.shape
    q3 = jnp.pad(qmla.reshape(B, MLA_HEADS, MLA_QCAT), ((0, 0), (0, 8 - MLA_HEADS), (0, 0)))
    per_seq = lambda b, c, pt: (b, 0, 0)
    in_specs = ([pl.BlockSpec((1, 8, MLA_QCAT), per_seq),
                 pl.BlockSpec((1, 1, MLA_KV_LORA), per_seq), pl.BlockSpec((1, 1, MLA_ROPE), per_seq),
                 pl.BlockSpec(wuv.shape, lambda b, c, pt: (0, 0, 0))]
                + _page_specs((None, None, PAGE_SIZE, MLA_KV_LORA), layer, B, n_pages, P)
                + _page_specs((None, None, MLA_ROPE, PAGE_SIZE), layer, B, n_pages, P))
    grid_spec = pltpu.PrefetchScalarGridSpec(
        num_scalar_prefetch=1, grid=(B, n_pages // P), in_specs=in_specs,
        out_specs=pl.BlockSpec((1, 1, MLA_HEADS * MLA_V), per_seq),
        scratch_shapes=[pltpu.VMEM((8, 1), F32), pltpu.VMEM((8, 1), F32), pltpu.VMEM((8, MLA_KV_LORA), F32)])
    kern = functools.partial(_mla_decode_kernel, P=P, scale=(MLA_NOPE + MLA_ROPE) ** -0.5)
    out = pl.pallas_call(
        kern, grid_spec=grid_spec, out_shape=jax.ShapeDtypeStruct((B, 1, MLA_HEADS * MLA_V), BF16),
        compiler_params=_cparams(2), name="mla_decode")(
            page_table.reshape(-1), q3, lat_new.reshape(B, 1, -1), kr_new.reshape(B, 1, -1), wuv,
            *([cache_lat] * P), *([cache_kr] * P))
    return out.reshape(B, -1)


def _fox_decode_kernel(pt_ref, q_ref, kn_ref, vn_ref, fn_ref, tri_ref, *rest, P):
    k_refs, v_refs, f_refs = rest[:P], rest[P:2 * P], rest[2 * P:3 * P]
    o_ref, m_ref, l_ref, acc_ref, carry_ref = rest[3 * P:]
    c = pl.program_id(1)
    rowh = _iota((8, FOX_D), 0)
    head_of_lane = lax.shift_right_logical(_iota((8, FOX_D), 1), 6)
    own = rowh == head_of_lane
    qbd = jnp.where(own, q_ref[0].astype(F32), 0.0).astype(BF16)

    @pl.when(c == 0)
    def _():
        _flash_init(m_ref, l_ref, acc_ref)
        carry_ref[...] = fn_ref[0]

    carry = carry_ref[...]
    s_list = []
    for i in reversed(range(P)):
        f = jnp.concatenate([f_refs[i][...], jnp.zeros((8 - FOX_HEADS, PAGE_SIZE), F32)], axis=0)
        decay = _dot_sel(f, tri_ref[...]) + carry
        s_list.append(_dot(qbd, k_refs[i][...].astype(BF16)) + decay)
        carry = carry + jnp.sum(f, axis=-1, keepdims=True)
    carry_ref[...] = carry
    s = jnp.concatenate(s_list, axis=1)
    m_prev = m_ref[...]
    m_new = jnp.maximum(m_prev, jnp.max(s, axis=-1, keepdims=True))
    alpha = jnp.exp(m_prev - m_new)
    p = jnp.exp(s - m_new)
    l_ref[...] = alpha * l_ref[...] + jnp.sum(p, axis=-1, keepdims=True)
    pb = p.astype(BF16)
    pv = None
    for n, i in enumerate(reversed(range(P))):
        t = _dot_nt(pb[:, n * PAGE_SIZE:(n + 1) * PAGE_SIZE], v_refs[i][...].astype(BF16))
        pv = t if pv is None else pv + t
    acc_ref[...] = alpha * acc_ref[...] + pv
    m_ref[...] = m_new

    @pl.when(c == pl.num_programs(1) - 1)
    def _():
        s_new = jnp.sum(qbd.astype(F32) * kn_ref[0], axis=-1, keepdims=True)
        m_prev = m_ref[...]
        m_new = jnp.maximum(m_prev, s_new)
        alpha = jnp.exp(m_prev - m_new)
        p_new = jnp.exp(s_new - m_new)
        l = alpha * l_ref[...] + p_new
        o = (alpha * acc_ref[...] + p_new * vn_ref[0]) / l
        o_ref[0] = jnp.sum(jnp.where(own, o, 0.0), axis=0, keepdims=True).astype(BF16)


def _fox_decode(layer, page_table, fq, k_new, v_new, f_new, tri, cache_k, cache_v, cache_ft, P):
    B, n_pages = page_table.shape
    per_seq = lambda b, c, pt: (b, 0, 0)
    in_specs = ([pl.BlockSpec((1, 1, FOX_D), per_seq)] * 3
                + [pl.BlockSpec((1, 8, 1), per_seq), pl.BlockSpec(tri.shape, lambda b, c, pt: (0, 0))]
                + _page_specs((None, None, FOX_D, PAGE_SIZE), layer, B, n_pages, P, reverse=True)
                + _page_specs((None, None, FOX_D, PAGE_SIZE), layer, B, n_pages, P, reverse=True)
                + _page_specs((None, None, FOX_HEADS, PAGE_SIZE), layer, B, n_pages, P, reverse=True))
    grid_spec = pltpu.PrefetchScalarGridSpec(
        num_scalar_prefetch=1, grid=(B, n_pages // P), in_specs=in_specs,
        out_specs=pl.BlockSpec((1, 1, FOX_D), per_seq),
        scratch_shapes=[pltpu.VMEM((8, 1), F32), pltpu.VMEM((8, 1), F32), pltpu.VMEM((8, FOX_D), F32),
                        pltpu.VMEM((8, 1), F32)])
    f_new8 = jnp.pad(f_new.reshape(B, FOX_HEADS, 1), ((0, 0), (0, 8 - FOX_HEADS), (0, 0)))
    out = pl.pallas_call(
        functools.partial(_fox_decode_kernel, P=P), grid_spec=grid_spec,
        out_shape=jax.ShapeDtypeStruct((B, 1, FOX_D), BF16),
        compiler_params=_cparams(2), name="fox_decode")(
            page_table.reshape(-1), fq.reshape(B, 1, -1), k_new.reshape(B, 1, -1), v_new.reshape(B, 1, -1),
            f_new8, tri, *([cache_k] * P), *([cache_v] * P), *([cache_ft] * P))
    return out.reshape(B, -1)


def _group_rows(q_row, g):
    half = lax.shift_right_logical(_iota((1, NSA_KV), 1), 6)
    qf = q_row.astype(F32)
    rows = [jnp.where(half == g, qf[:, j * NSA_KV:(j + 1) * NSA_KV], 0.0) for j in range(NSA_GROUP)]
    return jnp.concatenate(rows + [jnp.zeros((8 - NSA_GROUP, NSA_KV), F32)], axis=0).astype(BF16)


def _nsa_cmp_decode_kernel(pt_ref, q_ref, pk_ref, pv_ref, pair_ref, perm_ref, *rest, P, n_cmp, n_sel):
    xk_refs, xv_refs = rest[:P], rest[P:2 * P]
    ocmp_ref, sel_ref, kc_ref, vc_ref = rest[2 * P:]
    c = pl.program_id(1)
    per_page = PAGE_SIZE // BLK_CMP
    rows = pl.ds(pl.multiple_of(c * (P * per_page), P * per_page), P * per_page)
    perm = perm_ref[...]
    for refs, phi_ref, dst in ((xk_refs, pk_ref, kc_ref), (xv_refs, pv_ref, vc_ref)):
        xs = []
        for i in range(0, P, 2):
            pair = jnp.concatenate([refs[i][...], refs[i + 1][...]], axis=1).astype(BF16)
            xs.append(_dot_nt(perm, pair))
        acc = None
        for r in range(0, BLK_CMP, 2):
            lhs = jnp.concatenate(
                [jnp.concatenate([x[rr * 8:(rr + 1) * 8] for x in xs], axis=0) for rr in (r, r + 1)], axis=1)
            t = _dot(lhs.astype(BF16), phi_ref[r * NSA_KV:(r + 2) * NSA_KV, :])
            acc = t if acc is None else acc + t
        dst[rows, :] = acc

    @pl.when(c == pl.num_programs(1) - 1)
    def _():
        kc = kc_ref[...].astype(BF16)
        vc = vc_ref[...].astype(BF16)
        q_row = q_ref[0]
        lane = _iota((1, n_cmp // 2), 1)
        lane_f = lane.astype(F32)
        for g in range(NSA_KV_HEADS):
            qg = _group_rows(q_row, g)
            s = _dot_nt(qg, kc)
            p = jnp.exp(s - jnp.max(s, axis=-1, keepdims=True))
            pc = p / jnp.maximum(jnp.sum(p, axis=-1, keepdims=True), 1e-30)
            ocmp_ref[0, g] = _dot(pc.astype(BF16), vc)
            imp = jnp.sum(pc[0:NSA_GROUP], axis=0, keepdims=True)
            imp = _dot_sel(imp, pair_ref[...])
            vals = jnp.where((lane == 0) | (lane == n_cmp // 2 - 1), FORCE_SCORE, imp)
            picks = jnp.zeros((1, 128), jnp.int32)
            for it in range(n_sel):
                m = jnp.max(vals, axis=-1, keepdims=True)
                first = jnp.min(jnp.where(vals == m, lane_f, float(n_cmp)), axis=-1, keepdims=True)
                vals = jnp.where(lane_f == first, -2.0, vals)
                picks = jnp.where(_iota((1, 128), 1) == it, first.astype(jnp.int32), picks)
            sel_ref[0, g] = picks


def _nsa_cmp_decode(layer, page_table, nq, phik, phiv, pair, perm, cache_kt, cache_vt, P, n_sel):
    B, n_pages = page_table.shape
    n_cmp = n_pages * PAGE_SIZE // BLK_CMP
    per_seq = lambda b, c, pt: (b, 0, 0)
    const = lambda b, c, pt: (0, 0)
    in_specs = ([pl.BlockSpec((1, 1, NSA_Q), per_seq), pl.BlockSpec(phik.shape, const),
                 pl.BlockSpec(phiv.shape, const), pl.BlockSpec(pair.shape, const), pl.BlockSpec(perm.shape, const)]
                + _page_specs((None, None, NSA_KV, PAGE_SIZE), layer, B, n_pages, P)
                + _page_specs((None, None, NSA_KV, PAGE_SIZE), layer, B, n_pages, P))
    grid_spec = pltpu.PrefetchScalarGridSpec(
        num_scalar_prefetch=1, grid=(B, n_pages // P), in_specs=in_specs,
        out_specs=[pl.BlockSpec((1, NSA_KV_HEADS, 8, NSA_KV), lambda b, c, pt: (b, 0, 0, 0)),
                   pl.BlockSpec((1, NSA_KV_HEADS, 1, 128), lambda b, c, pt: (b, 0, 0, 0))],
        scratch_shapes=[pltpu.VMEM((n_cmp, NSA_KV), F32), pltpu.VMEM((n_cmp, NSA_KV), F32)])
    kern = functools.partial(_nsa_cmp_decode_kernel, P=P, n_cmp=n_cmp, n_sel=n_sel)
    return pl.pallas_call(
        kern, grid_spec=grid_spec,
        out_shape=[jax.ShapeDtypeStruct((B, NSA_KV_HEADS, 8, NSA_KV), F32),
                   jax.ShapeDtypeStruct((B, NSA_KV_HEADS, 1, 128), jnp.int32)],
        compiler_params=_cparams(2), name="nsa_cmp_decode")(
            page_table.reshape(-1), nq.reshape(B, 1, -1), phik, phiv, pair, perm,
            *([cache_kt] * P), *([cache_vt] * P))


def _softmax_parts(parts):
    m = parts[0].max(axis=-1, keepdims=True)
    for s in parts[1:]:
        m = jnp.maximum(m, s.max(axis=-1, keepdims=True))
    es = [jnp.exp(s - m) for s in parts]
    tot = es[0].sum(axis=-1, keepdims=True)
    for e in es[1:]:
        tot = tot + e.sum(axis=-1, keepdims=True)
    inv = 1.0 / jnp.maximum(tot, 1e-30)
    return [e * inv for e in es]


def _nsa_sel_decode_kernel(pt_ref, sel_ref, q_ref, gate_ref, ocmp_ref, skn_ref, svn_ref, wkn_ref, wvn_ref,
                           wknt_ref, wvnt_ref, wk_ref, wv_ref, egate_ref, *rest, n_sel):
    nb = NSA_KV_HEADS * n_sel
    k_refs, v_refs = rest[:nb], rest[nb:2 * nb]
    o_ref, wko_ref, wvo_ref = rest[2 * nb:]
    b = pl.program_id(0)
    q_row = q_ref[0]
    skn, svn, wkn, wvn = skn_ref[0], svn_ref[0], wkn_ref[0], wvn_ref[0]
    wk = wk_ref[0, 0]
    wv = wv_ref[0, 0]
    key_half = lax.shift_right_logical(_iota((8, PAGE_SIZE), 1), 6)
    o_slc, o_win = [], []
    for g in range(NSA_KV_HEADS):
        qg = _group_rows(q_row, g)
        qf = qg.astype(F32)
        parts = []
        for n in range(n_sel):
            blk = sel_ref[b * nb + g * n_sel + n]
            s = _dot(qg, k_refs[g * n_sel + n][...].astype(BF16))
            parts.append(jnp.where(key_half == (blk & 1), s, NEG_INF))
        parts.append(jnp.sum(qf * skn, axis=-1, keepdims=True))
        ps = _softmax_parts(parts)
        o = ps[n_sel] * svn
        for n in range(n_sel):
            o = o + _dot_nt(ps[n].astype(BF16), v_refs[g * n_sel + n][...].astype(BF16))
        o_slc.append(o)
        parts = [_dot(qg, wk.astype(BF16)), jnp.sum(qf * wkn, axis=-1, keepdims=True)]
        ps = _softmax_parts(parts)
        o_win.append(_dot_nt(ps[0].astype(BF16), wv.astype(BF16)) + ps[1] * wvn)
    gexp = _dot_sel(gate_ref[0], egate_ref[...])
    low = _iota((1, NSA_KV), 1) < HEAD_DIM

    def slots(o0, o1):
        return jnp.concatenate([jnp.where(low, o0[j:j + 1], o1[j:j + 1]) for j in range(NSA_GROUP)], axis=1)
    out = (gexp[:, 0:NSA_Q] * slots(ocmp_ref[0, 0], ocmp_ref[0, 1])
           + gexp[:, NSA_Q:2 * NSA_Q] * slots(o_slc[0], o_slc[1])
           + gexp[:, 2 * NSA_Q:] * slots(o_win[0], o_win[1]))
    o_ref[0] = out.astype(BF16)
    nw = wk.shape[1]
    last = _iota(wk.shape, 1) == nw - 1
    mine = _iota(wknt_ref.shape, 1) == b
    for src, newt_ref, dst in ((wk, wknt_ref, wko_ref), (wv, wvnt_ref, wvo_ref)):
        col = jnp.sum(jnp.where(mine, newt_ref[...], 0.0), axis=1, keepdims=True)
        dst[0] = jnp.where(last, col, pltpu.roll(src, nw - 1, axis=1))


def _nsa_sel_decode(layer, page_table, sel, nq, ngate, ocmp, new_rows, win_kt, win_vt, egate, cache_kt, cache_vt,
                    n_sel):
    B, n_pages = page_table.shape
    nw = win_kt.shape[3]
    halves = PAGE_SIZE // BLK_SLC
    per_seq = lambda b, pt, sl: (b, 0, 0)
    const = lambda b, pt, sl: (0, 0)

    def blk_spec(n):
        def index_map(b, pt, sl):
            bb = jnp.minimum(b, B - 1)
            blk = jnp.clip(sl[bb * (NSA_KV_HEADS * n_sel) + n], 0, n_pages * halves - 1)
            return (layer, pt[bb * n_pages + blk // halves], 0, 0)
        return pl.BlockSpec((None, None, NSA_KV, PAGE_SIZE), index_map)
    blk_specs = [blk_spec(n) for n in range(NSA_KV_HEADS * n_sel)]
    win_spec = pl.BlockSpec((1, 1, NSA_KV, nw), lambda b, pt, sl: (layer, b, 0, 0))
    new_t = [jnp.transpose(new_rows[2]), jnp.transpose(new_rows[3])]
    in_specs = ([pl.BlockSpec((1, 1, NSA_Q), per_seq), pl.BlockSpec((1, 1, 3 * NSA_HEADS), per_seq),
                 pl.BlockSpec((1, NSA_KV_HEADS, 8, NSA_KV), lambda b, pt, sl: (b, 0, 0, 0))]
                + [pl.BlockSpec((1, 1, NSA_KV), per_seq)] * 4
                + [pl.BlockSpec((NSA_KV, B), const)] * 2
                + [win_spec, win_spec, pl.BlockSpec(egate.shape, const)]
                + blk_specs + blk_specs)
    grid_spec = pltpu.PrefetchScalarGridSpec(
        num_scalar_prefetch=2, grid=(B,), in_specs=in_specs,
        out_specs=[pl.BlockSpec((1, 1, NSA_Q), per_seq), pl.BlockSpec((1, NSA_KV, nw), per_seq),
                   pl.BlockSpec((1, NSA_KV, nw), per_seq)])
    out, wko, wvo = pl.pallas_call(
        functools.partial(_nsa_sel_decode_kernel, n_sel=n_sel), grid_spec=grid_spec,
        out_shape=[jax.ShapeDtypeStruct((B, 1, NSA_Q), BF16), jax.ShapeDtypeStruct((B, NSA_KV, nw), F32),
                   jax.ShapeDtypeStruct((B, NSA_KV, nw), F32)],
        compiler_params=_cparams(1), name="nsa_sel_decode")(
            page_table.reshape(-1), sel.reshape(-1), nq.reshape(B, 1, -1), ngate.reshape(B, 1, -1), ocmp,
            *[r.reshape(B, 1, -1) for r in new_rows], *new_t, win_kt, win_vt, egate,
            *([cache_kt] * (NSA_KV_HEADS * n_sel)), *([cache_vt] * (NSA_KV_HEADS * n_sel)))
    return out.reshape(B, -1), wko, wvo


def _cumsum_kernel(x_ref, o_ref):
    x = x_ref[...]
    n = x.shape[-1]
    lane = _iota(x.shape, 1)
    shift = 1
    while shift < n:
        x = x + jnp.where(lane >= shift, pltpu.roll(x, shift, axis=1), 0.0)
        shift *= 2
    o_ref[...] = x


def _cumsum_lanes(x):
    return pl.pallas_call(_cumsum_kernel, out_shape=jax.ShapeDtypeStruct(x.shape, F32), name="logf_cumsum")(x)


def _rot_cols(w, d):
    w3 = w.reshape(w.shape[0], -1, d)
    return jnp.concatenate([-w3[..., d // 2:], w3[..., :d // 2]], axis=-1).reshape(w.shape)


def _slot_cols(w):
    n = w.shape[0]
    return w.reshape(n, NSA_KV_HEADS, NSA_GROUP, HEAD_DIM).transpose(0, 2, 1, 3).reshape(n, NSA_Q)


def _layer_weights(w_in, b_forget, g_mla_q, g_mla_kv, w_mla_uq, w_mla_uk, w_mla_uv, phi_k, phi_v,
                   w_br_mla, w_br_nsa, w_br_fox, w_out, g_mix, g_ffn, w_ff1, w_ff2):
    o = 0
    cols = {}
    for name, n in (('cq', MLA_Q_LORA), ('ckv', MLA_KV_LORA), ('kr', MLA_ROPE), ('nq', NSA_Q), ('nkv', 6 * NSA_KV),
                    ('ng', 3 * NSA_HEADS), ('fqkv', 3 * FOX_D), ('ff', FOX_HEADS), ('mg', N_BRANCH * D_MODEL)):
        cols[name] = w_in[:, o:o + n]
        o += n
    pad = jnp.zeros((D_MODEL, 128 - 3 * NSA_HEADS - FOX_HEADS), F32)
    wsm = jnp.concatenate([cols['ng'], cols['ff'], pad], axis=1)
    wkr = jnp.concatenate([jnp.tile(cols['kr'], (1, MLA_HEADS)),
                           jnp.tile(_rot_cols(cols['kr'], MLA_ROPE), (1, MLA_HEADS))], axis=1)
    nq = _slot_cols(cols['nq'])
    nkv = cols['nkv']
    nk = jnp.concatenate([nkv[:, i * NSA_KV:(i + 1) * NSA_KV] for i in (0, 2, 4)], axis=1)
    uq = w_mla_uq.reshape(MLA_Q_LORA, MLA_HEADS, MLA_NOPE + MLA_ROPE)
    uq_nope = uq[:, :, :MLA_NOPE].reshape(MLA_Q_LORA, -1)
    uq_rope = uq[:, :, MLA_NOPE:].reshape(MLA_Q_LORA, -1)
    eye_h = jnp.eye(MLA_HEADS, dtype=F32)
    wuk = jnp.einsum('chn,hk->hnkc', w_mla_uk, eye_h).reshape(MLA_HEADS * MLA_NOPE, MLA_HEADS * MLA_KV_LORA)
    wuv = jnp.einsum('chv,hk->hckv', w_mla_uv, eye_h).reshape(MLA_HEADS, MLA_KV_LORA, MLA_HEADS * MLA_V)
    eye_g = jnp.eye(NSA_KV_HEADS, dtype=F32)

    def phi_big(phi):
        p3 = phi.reshape(BLK_CMP, HEAD_DIM, HEAD_DIM)
        return jnp.einsum('rde,gk->rgdke', p3, eye_g).reshape(CMP_FLAT, NSA_KV).astype(BF16)
    bias = b_forget.reshape(1, FOX_HEADS)
    proj = dict(
        gmix=g_mix.reshape(1, -1), wa=jnp.concatenate([cols['cq'], cols['ckv']], axis=1).astype(BF16),
        wkr=wkr.astype(BF16), wsm=wsm.astype(BF16), wnq=jnp.concatenate([nq, _rot_cols(nq, HEAD_DIM)], axis=1).astype(BF16),
        wnkv=jnp.concatenate([nkv, _rot_cols(nk, HEAD_DIM)], axis=1).astype(BF16),
        wfox=cols['fqkv'].astype(BF16), gq=g_mla_q.reshape(1, -1), gkv=g_mla_kv.reshape(1, -1),
        wuq=jnp.concatenate([uq_nope, uq_rope, _rot_cols(uq_rope, MLA_ROPE)], axis=1).astype(BF16),
        wuk=wuk.astype(BF16), bias=bias)
    merge = dict(
        gmix=g_mix.reshape(1, -1), wmg=cols['mg'].astype(BF16), wbm=w_br_mla.astype(BF16),
        wbn=_slot_cols(w_br_nsa.T).T.astype(BF16), wbf=w_br_fox.astype(BF16), wout=w_out.astype(BF16))
    ffn = dict(g=g_ffn.reshape(1, -1), w1=w_ff1.astype(BF16), w2=w_ff2.astype(BF16))
    return dict(proj=proj, merge=merge, ffn=ffn, wuv=wuv.astype(BF16), phik=phi_big(phi_k), phiv=phi_big(phi_v))


def _rope_table(pos):
    def cs(d, reps):
        inv = ROPE_THETA ** (-jnp.arange(0, d, 2, dtype=F32) / d)
        ang = pos.astype(F32)[:, None] * inv[None, :]
        c, s = jnp.cos(ang), jnp.sin(ang)
        return jnp.tile(jnp.concatenate([c, c], axis=1), (1, reps)), jnp.tile(jnp.concatenate([s, s], axis=1), (1, reps))
    c64, s64 = cs(HEAD_DIM, NSA_HEADS)
    c32, s32 = cs(MLA_ROPE, MLA_HEADS)
    return jnp.concatenate([c64, s64, c32, s32], axis=1)


def _pair_permutation():
    per_page = PAGE_SIZE // BLK_CMP
    row = jnp.arange(2 * PAGE_SIZE)
    r, page, blk = row // (2 * per_page), (row // per_page) % 2, row % per_page
    src = page * PAGE_SIZE + blk * BLK_CMP + r
    return (src[:, None] == jnp.arange(2 * PAGE_SIZE)[None, :]).astype(BF16)


def _selection_constants(S):
    t = ATT_TILE
    gate_col = jnp.arange(3 * NSA_HEADS)
    lane = jnp.arange(3 * NSA_Q)
    br, rem = lane // NSA_Q, lane % NSA_Q
    head = (rem % NSA_KV) // HEAD_DIM * NSA_GROUP + rem // NSA_KV
    egate = (gate_col[:, None] == (head * 3 + br)[None, :]).astype(BF16)
    out = dict(egate=egate)
    if S is not None:
        n_cmp, n_slc = S // BLK_CMP, S // BLK_SLC
        out['pair'] = (jnp.arange(n_cmp)[:, None] // (BLK_SLC // BLK_CMP) == jnp.arange(n_slc)[None, :]).astype(BF16)
        key_blk = jnp.arange(S) // BLK_SLC
        eblk = (jnp.arange(n_slc)[:, None] == key_blk[None, :]).astype(BF16)
        out['eblk'] = eblk.reshape(n_slc, S // t, t).transpose(1, 0, 2)
    return out


def _token_tile(T):
    for tm in (512, 256, 128):
        if T % tm == 0:
            return tm
    return T


def _prompt_mixer(x, lw, tab, consts, B, S):
    tm = min(_token_tile(B * S), S)
    pr = _project(x, tab, S, lw['proj'], tm)
    o_mla = _mla_prompt(pr['qmla'], pr['kcat'], lw['wuv'], B, S)
    t = ATT_TILE
    logf_t = pr['fox_logf'].reshape(B, S, FOX_HEADS).transpose(0, 2, 1).reshape(B * FOX_HEADS, S)
    cum = _cumsum_lanes(logf_t).reshape(B, FOX_HEADS, S)
    cum_q = cum.transpose(0, 2, 1).reshape(B * S, FOX_HEADS)
    cum_k = cum.reshape(B, FOX_HEADS, S // t, t).transpose(0, 2, 1, 3)
    o_fox = _fox_prompt(pr['fq'], pr['fox_k'], pr['fox_v'], cum_q, cum_k, B, S)
    n_blk = B * S // BLK_CMP
    kc, vc = _compress(pr['nsa_cmp_k'].reshape(n_blk, CMP_FLAT), pr['nsa_cmp_v'].reshape(n_blk, CMP_FLAT),
                       lw['phik'], lw['phiv'])
    o_nsa = _nsa_prompt(pr['nq'], pr['ngate'], kc, vc, pr['nsa_slc_k'], pr['nsa_slc_v'],
                        pr['nsa_win_k'], pr['nsa_win_v'], consts, B, S)
    x1 = _merge(x, o_mla, o_nsa, o_fox, lw['merge'], tm)
    return x1, pr


def _sample_mixer(x, lw, tab, consts, layer, page_table, caches):
    B = x.shape[0]
    n_pages = page_table.shape[1]
    pr = _project(x, tab, B, lw['proj'], B)
    o_mla = _mla_decode(layer, page_table, pr['qmla'], pr['mla_latent'], pr['mla_krope'], lw['wuv'],
                        caches['mla_latent'], caches['mla_krope'], min(MLA_PAGES_PER_STEP, n_pages))
    o_fox = _fox_decode(layer, page_table, pr['fq'], pr['fox_k'], pr['fox_v'], pr['fox_logf'], consts['tri'],
                        caches['fox_k'], caches['fox_v'], caches['fox_logf_t'], min(FOX_PAGES_PER_STEP, n_pages))
    n_sel = TOP_N - 1
    ocmp, sel = _nsa_cmp_decode(layer, page_table, pr['nq'], lw['phik'], lw['phiv'], consts['pair_dec'],
                                consts['perm'], caches['nsa_cmp_k'], caches['nsa_cmp_v'],
                                min(CMP_PAGES_PER_STEP, n_pages), n_sel)
    sel = sel[:, :, 0, :n_sel]
    new_rows = (pr['nsa_slc_k'], pr['nsa_slc_v'], pr['nsa_win_k'], pr['nsa_win_v'])
    o_nsa, wko, wvo = _nsa_sel_decode(layer, page_table, sel, pr['nq'], pr['ngate'], ocmp, new_rows,
                                      caches['nsa_win_k'], caches['nsa_win_v'], consts['egate'],
                                      caches['nsa_slc_k'], caches['nsa_slc_v'], n_sel)
    x1 = _merge(x, o_mla, o_nsa, o_fox, lw['merge'], B)
    pr = dict(pr, nsa_win_k=wko, nsa_win_v=wvo)
    return x1, pr


STATE_NAMES = ('mla_latent', 'mla_krope', 'nsa_cmp_k', 'nsa_cmp_v', 'nsa_slc_k', 'nsa_slc_v', 'nsa_win_k', 'nsa_win_v',
               'fox_k', 'fox_v', 'fox_logf')


def kernel(x_prompt, x_sample, cache_mla_latent, cache_mla_krope, cache_nsa_cmp_k, cache_nsa_cmp_v, cache_nsa_slc_k, cache_nsa_slc_v, state_nsa_win_k, state_nsa_win_v, cache_fox_k, cache_fox_v, cache_fox_logf, page_table, w_in, b_forget, g_mla_q, g_mla_kv, w_mla_uq, w_mla_uk, w_mla_uv, w_nsa_phi_k, w_nsa_phi_v, w_br_mla, w_br_nsa, w_br_fox, w_out, g_norm_mix, g_norm_ffn, w_ff1, w_ff2, g_norm_final):
    B, S, _ = x_prompt.shape
    Bs, Ss, _ = x_sample.shape
    depth = w_in.shape[0]
    n_pages = page_table.shape[1]
    past = n_pages * PAGE_SIZE
    nw = state_nsa_win_k.shape[2]
    assert Ss == 1 and nw == WINDOW and S % ATT_TILE == 0 and S >= WINDOW
    assert past // BLK_SLC >= TOP_N and past // BLK_CMP <= 2 * 128

    def keys_on_lanes(c):
        return jnp.transpose(c, (0, 1, 3, 4, 2)).reshape(c.shape[0], c.shape[1], c.shape[3] * c.shape[4], c.shape[2])
    caches = {
        'mla_latent': cache_mla_latent, 'mla_krope': jnp.swapaxes(cache_mla_krope, 2, 3),
        'nsa_cmp_k': keys_on_lanes(cache_nsa_cmp_k), 'nsa_cmp_v': keys_on_lanes(cache_nsa_cmp_v),
        'nsa_slc_k': keys_on_lanes(cache_nsa_slc_k), 'nsa_slc_v': keys_on_lanes(cache_nsa_slc_v),
        'nsa_win_k': keys_on_lanes(state_nsa_win_k), 'nsa_win_v': keys_on_lanes(state_nsa_win_v),
        'fox_k': keys_on_lanes(cache_fox_k), 'fox_v': keys_on_lanes(cache_fox_v),
        'fox_logf_t': jnp.swapaxes(cache_fox_logf, 2, 3),
    }
    tab_p = _rope_table(jnp.arange(S, dtype=jnp.int32))
    tab_s = _rope_table(jnp.full((Bs,), past, dtype=jnp.int32))
    consts_p = _selection_constants(S)
    consts_s = dict(
        egate=consts_p['egate'],
        pair_dec=(jnp.arange(past // BLK_CMP)[:, None] // (BLK_SLC // BLK_CMP)
                  == jnp.arange(past // BLK_SLC)[None, :]).astype(BF16),
        tri=(jnp.arange(PAGE_SIZE)[:, None] > jnp.arange(PAGE_SIZE)[None, :]).astype(BF16),
        perm=_pair_permutation())

    xp = x_prompt.reshape(B * S, D_MODEL)
    xs = x_sample.reshape(Bs, D_MODEL)
    gfin = g_norm_final.reshape(1, -1)
    rows_p = {n: [] for n in STATE_NAMES}
    rows_s = {n: [] for n in STATE_NAMES}
    yp = ys = None
    for layer in range(depth):
        lw = _layer_weights(w_in[layer], b_forget[layer], g_mla_q[layer], g_mla_kv[layer], w_mla_uq[layer],
                            w_mla_uk[layer], w_mla_uv[layer], w_nsa_phi_k[layer], w_nsa_phi_v[layer],
                            w_br_mla[layer], w_br_nsa[layer], w_br_fox[layer], w_out[layer],
                            g_norm_mix[layer], g_norm_ffn[layer], w_ff1[layer], w_ff2[layer])
        xp, pr_p = _prompt_mixer(xp, lw, tab_p, consts_p, B, S)
        xp, yp = _ffn(xp, lw['ffn']['g'], lw['ffn']['w1'], lw['ffn']['w2'], gfin, _token_tile(B * S))
        xs, pr_s = _sample_mixer(xs, lw, tab_s, consts_s, layer, page_table, caches)
        xs, ys = _ffn(xs, lw['ffn']['g'], lw['ffn']['w1'], lw['ffn']['w2'], gfin, Bs)
        for n in STATE_NAMES:
            rows_p[n].append(pr_p[n])
            rows_s[n].append(pr_s[n])

    def stack_p(n):
        a = jnp.stack(rows_p[n]).reshape(depth, B, S, -1)
        if n in ('nsa_win_k', 'nsa_win_v'):
            a = a[:, :, S - min(WINDOW, S):]
        if n.startswith('nsa_'):
            return a.reshape(a.shape[:3] + (NSA_KV_HEADS, HEAD_DIM))
        if n in ('fox_k', 'fox_v'):
            return a.reshape(a.shape[:3] + (FOX_HEADS, HEAD_DIM))
        return a

    def stack_s(n):
        a = jnp.stack(rows_s[n])
        if n in ('nsa_win_k', 'nsa_win_v'):
            return jnp.transpose(a.reshape(depth, Bs, NSA_KV_HEADS, HEAD_DIM, nw), (0, 1, 4, 2, 3))
        a = a.reshape(depth, Bs, 1, -1)
        if n.startswith('nsa_'):
            return a.reshape(a.shape[:3] + (NSA_KV_HEADS, HEAD_DIM))
        if n in ('fox_k', 'fox_v'):
            return a.reshape(a.shape[:3] + (FOX_HEADS, HEAD_DIM))
        return a

    outs = [yp.reshape(B, S, D_MODEL), ys.reshape(Bs, 1, D_MODEL)]
    for n in STATE_NAMES:
        outs += [stack_p(n), stack_s(n)]
    return tuple(outs)
```
